```python
import jax, jax.numpy as jnp
from jax import lax
import numpy as np

D_MODEL = 1024
BATCH = 32
SEQ = 2048
DEPTH = 1

CHUNK = 64
Q_BLOCK = 128
N_MEM = 256
EPS = 1e-6

HG_HEADS = 4
HG_DK = 128
HG_DV = 128
HG_WIDTH = HG_HEADS * HG_DK
FOX_HEADS = 8
FOX_DH = 64
FOX_WIDTH = FOX_HEADS * FOX_DH
MEM_HEADS = 4
MEM_DH = 128
MEM_WIDTH = MEM_HEADS * MEM_DH
N_BRANCH = 3
D_FF = 2816
CONV_W = 3

IN_SPLITS = (HG_WIDTH, HG_WIDTH, HG_WIDTH, HG_WIDTH,
             FOX_WIDTH, FOX_WIDTH, FOX_WIDTH, FOX_HEADS,
             MEM_WIDTH, N_BRANCH * D_MODEL)
IN_COLS = 4 * HG_WIDTH + 3 * FOX_WIDTH + FOX_HEADS + MEM_WIDTH + N_BRANCH * D_MODEL

kernel_name = "hybrid_hgrn2_fox_memory_convglu"


def _split_points():
    pts, acc = [], 0
    for s in IN_SPLITS[:-1]:
        acc += s
        pts.append(acc)
    return pts


def _rmsnorm(x, g):
    xf = x.astype(jnp.float32)
    return xf * lax.rsqrt(jnp.mean(xf * xf, axis=-1, keepdims=True) + EPS) * g.astype(jnp.float32)


def _hgrn2_mixer(q, f_logit, i, g_out, lb, norm_g):
    B, T, _ = q.shape
    n = T // CHUNK

    def heads(z):
        return z.reshape(B, n, CHUNK, HG_HEADS, -1).transpose(1, 0, 3, 2, 4)

    f = lb + (1.0 - lb) * jax.nn.sigmoid(f_logit.astype(jnp.float32))
    qh = heads(jax.nn.silu(q.astype(jnp.float32)))
    kh = heads(1.0 - f)
    ih = heads(i.astype(jnp.float32))
    G = jnp.cumsum(heads(jnp.log(f)), axis=3)
    causal = jnp.tril(jnp.ones((CHUNK, CHUNK), dtype=bool))[:, :, None]

    def step(S, inp):
        qc, kc, ic, Gc = inp
        diff = Gc[:, :, :, None, :] - Gc[:, :, None, :, :]
        decay = jnp.exp(jnp.where(causal, diff, -jnp.inf))
        A = jnp.einsum('bhtc,bhsc,bhtsc->bhts', qc, kc, decay)
        o = (jnp.einsum('bhts,bhsv->bhtv', A, ic)
             + jnp.einsum('bhtc,bhcv->bhtv', qc * jnp.exp(Gc), S))
        G_last = Gc[:, :, -1, :]
        S_new = (S * jnp.exp(G_last)[..., None]
                 + jnp.einsum('bhsc,bhsv->bhcv', kc * jnp.exp(G_last[:, :, None, :] - Gc), ic))
        return S_new, o

    S0 = jnp.zeros((B, HG_HEADS, HG_DK, HG_DV), jnp.float32)
    _, o = lax.scan(step, S0, (qh, kh, ih, G))
    o = o.transpose(1, 0, 3, 2, 4).reshape(B, T, HG_HEADS, HG_DV)
    o = _rmsnorm(o, norm_g).reshape(B, T, HG_WIDTH)
    return o * jax.nn.silu(g_out.astype(jnp.float32))


def _fox_mixer(q, k, v, f_logit, f_bias, q_g, k_g):
    B, T, _ = q.shape
    qh = _rmsnorm(q.reshape(B, T, FOX_HEADS, FOX_DH), q_g).transpose(0, 2, 1, 3)
    kh = _rmsnorm(k.reshape(B, T, FOX_HEADS, FOX_DH), k_g).transpose(0, 2, 1, 3)
    vh = v.astype(jnp.float32).reshape(B, T, FOX_HEADS, FOX_DH).transpose(0, 2, 1, 3)
    log_f = jax.nn.log_sigmoid(f_logit.astype(jnp.float32) + f_bias.astype(jnp.float32))
    Fc = jnp.cumsum(log_f, axis=1).transpose(0, 2, 1)
    scale = FOX_DH ** -0.5
    outs = []
    for blk in range(T // Q_BLOCK):
        lo, hi = blk * Q_BLOCK, (blk + 1) * Q_BLOCK
        s = (jnp.einsum('bhqd,bhkd->bhqk', qh[:, :, lo:hi], kh[:, :, :hi]) * scale
             + Fc[:, :, lo:hi, None] - Fc[:, :, None, :hi])
        mask = (lo + jnp.arange(Q_BLOCK))[:, None] >= jnp.arange(hi)[None, :]
        p = jax.nn.softmax(jnp.where(mask, s, -jnp.inf), axis=-1)
        outs.append(jnp.einsum('bhqk,bhkd->bhqd', p, vh[:, :, :hi]))
    o = jnp.concatenate(outs, axis=2)
    return o.transpose(0, 2, 1, 3).reshape(B, T, FOX_WIDTH)


def _memory_mixer(q, mem_kv, q_g, k_g):
    B, T, _ = q.shape
    M = mem_kv.shape[1]
    qh = _rmsnorm(q.reshape(B, T, MEM_HEADS, MEM_DH), q_g)
    mk, mv = jnp.split(mem_kv, 2, axis=-1)
    kh = _rmsnorm(mk.reshape(B, M, MEM_HEADS, MEM_DH), k_g)
    vh = mv.astype(jnp.float32).reshape(B, M, MEM_HEADS, MEM_DH)
    s = jnp.einsum('bthd,bmhd->bhtm', qh, kh) * (MEM_DH ** -0.5)
    p = jax.nn.softmax(s, axis=-1)
    return jnp.einsum('bhtm,bmhd->bthd', p, vh).reshape(B, T, MEM_WIDTH)


def _conv_glu_ffn(h, w_up, conv_w, conv_b, w_down):
    T = h.shape[1]
    a, v = jnp.split(h @ w_up.astype(jnp.float32), 2, axis=-1)
    ap = jnp.pad(a, ((0, 0), (CONV_W - 1, 0), (0, 0)))
    a = sum(ap[:, j:j + T] * conv_w[j].astype(jnp.float32) for j in range(CONV_W)) + conv_b.astype(jnp.float32)
    return (jax.nn.gelu(a, approximate=False) * v) @ w_down.astype(jnp.float32)


def setup_inputs(seed: int = 0) -> dict:
    key = jax.random.key(seed)
    ks = jax.random.split(key, 24)
    f32 = jnp.float32
    L = DEPTH

    def nrm(k, shape, fan_in):
        return jax.random.normal(k, shape, f32) * (fan_in ** -0.5)

    def gain(k, shape):
        return 1.0 + 0.02 * jax.random.normal(k, shape, f32)

    return {
        "x": jax.random.normal(ks[0], (BATCH, SEQ, D_MODEL), f32),
        "mem": jax.random.normal(ks[1], (BATCH, N_MEM, D_MODEL), f32),
        "norm_mix_g": gain(ks[2], (L, D_MODEL)),
        "norm_mem_g": gain(ks[3], (L, D_MODEL)),
        "w_in": nrm(ks[4], (L, D_MODEL, IN_COLS), D_MODEL),
        "hgrn_lb_logits": 0.1 * jax.random.normal(ks[5], (L + 1, HG_WIDTH), f32),
        "hgrn_norm_g": gain(ks[6], (L, HG_DV)),
        "fox_f_bias": 1.0 + 0.1 * jax.random.normal(ks[7], (L, FOX_HEADS), f32),
        "fox_q_norm_g": gain(ks[8], (L, FOX_DH)),
        "fox_k_norm_g": gain(ks[9], (L, FOX_DH)),
        "mem_kv_w": nrm(ks[10], (L, D_MODEL, 2 * MEM_WIDTH), D_MODEL),
        "mem_q_norm_g": gain(ks[11], (L, MEM_DH)),
        "mem_k_norm_g": gain(ks[12], (L, MEM_DH)),
        "w_br_hgrn": nrm(ks[13], (L, HG_WIDTH, D_MODEL), HG_WIDTH),
        "w_br_fox": nrm(ks[14], (L, FOX_WIDTH, D_MODEL), FOX_WIDTH),
        "w_br_mem": nrm(ks[15], (L, MEM_WIDTH, D_MODEL), MEM_WIDTH),
        "w_out": nrm(ks[16], (L, D_MODEL, D_MODEL), D_MODEL),
        "norm_ffn_g": gain(ks[17], (L, D_MODEL)),
        "ffn_w_up": nrm(ks[18], (L, D_MODEL, 2 * D_FF), D_MODEL),
        "ffn_conv_w": nrm(ks[19], (L, CONV_W, D_FF), CONV_W),
        "ffn_conv_b": 0.02 * jax.random.normal(ks[20], (L, D_FF), f32),
        "ffn_w_down": nrm(ks[21], (L, D_FF, D_MODEL), D_FF),
    }


def reference(x, mem, norm_mix_g, norm_mem_g, w_in, hgrn_lb_logits, hgrn_norm_g, fox_f_bias,
              fox_q_norm_g, fox_k_norm_g, mem_kv_w, mem_q_norm_g, mem_k_norm_g,
              w_br_hgrn, w_br_fox, w_br_mem, w_out, norm_ffn_g, ffn_w_up, ffn_conv_w,
              ffn_conv_b, ffn_w_down):
    B, T, _ = x.shape
    lower_bounds = jnp.cumsum(jax.nn.softmax(hgrn_lb_logits.astype(jnp.float32), axis=0), axis=0)
    pts = _split_points()
    for l in range(DEPTH):
        h = _rmsnorm(x, norm_mix_g[l])
        z = h @ w_in[l].astype(jnp.float32)
        (hq, hf, hi, hg, fq, fk, fv, ff, mq, gate_logits) = jnp.split(z, pts, axis=-1)
        y_a = _hgrn2_mixer(hq, hf, hi, hg, lower_bounds[l], hgrn_norm_g[l])
        y_b = _fox_mixer(fq, fk, fv, ff, fox_f_bias[l], fox_q_norm_g[l], fox_k_norm_g[l])
        mem_kv = _rmsnorm(mem, norm_mem_g[l]) @ mem_kv_w[l].astype(jnp.float32)
        y_c = _memory_mixer(mq, mem_kv, mem_q_norm_g[l], mem_k_norm_g[l])
        gates = jax.nn.sigmoid(gate_logits).reshape(B, T, N_BRANCH, D_MODEL)
        merged = (gates[:, :, 0] * (y_a @ w_br_hgrn[l].astype(jnp.float32))
                  + gates[:, :, 1] * (y_b @ w_br_fox[l].astype(jnp.float32))
                  + gates[:, :, 2] * (y_c @ w_br_mem[l].astype(jnp.float32)))
        x = x + (merged @ w_out[l].astype(jnp.float32)).astype(x.dtype)
        h2 = _rmsnorm(x, norm_ffn_g[l])
        x = x + _conv_glu_ffn(h2, ffn_w_up[l], ffn_conv_w[l], ffn_conv_b[l], ffn_w_down[l]).astype(x.dtype)
    return x
```

```python
import functools

import jax
import jax.numpy as jnp
from jax import lax
from jax.experimental import pallas as pl
from jax.experimental.pallas import tpu as pltpu

F32 = jnp.float32
BF16 = jnp.bfloat16

D_MODEL = 1024
EPS = 1e-6
CHUNK = 64
SUB = 16
HG_HEADS = 4
HG_D = 128
HG_WIDTH = HG_HEADS * HG_D
FOX_HEADS = 8
FOX_DH = 64
FOX_WIDTH = FOX_HEADS * FOX_DH
MEM_HEADS = 4
MEM_DH = 128
MEM_WIDTH = MEM_HEADS * MEM_DH
N_BRANCH = 3
D_FF = 2816
CONV_W = 3

LANES = 128
SUBLANES = 8
VMEM_LIMIT = 56 * 1024 * 1024

NEG_BIG = -1e30


def _const_spec(shape):
    nd = len(shape)
    return pl.BlockSpec(shape, lambda *_: (0,) * nd, pipeline_mode=pl.Buffered(1))


def _params(sem):
    return pltpu.CompilerParams(dimension_semantics=sem, vmem_limit_bytes=VMEM_LIMIT)


def _rms_rows(x, g):
    return x * lax.rsqrt(jnp.mean(x * x, axis=-1, keepdims=True) + EPS) * g


def _sigmoid(x):
    return 1.0 / (1.0 + jnp.exp(-x))


def _silu(x):
    return x * _sigmoid(x)


def _inproj_kernel(x_ref, g_ref, whg_ref, wfox_ref, wmq_ref, wgate_ref, wff_ref,
                   zhg_ref, zfox_ref, zmq_ref, zgate_ref, zff_ref, *, tn):
    h = _rms_rows(x_ref[...], g_ref[...]).astype(BF16)
    for w_ref, o_ref in ((whg_ref, zhg_ref), (wfox_ref, zfox_ref), (wmq_ref, zmq_ref),
                         (wgate_ref, zgate_ref), (wff_ref, zff_ref)):
        n = w_ref.shape[1]
        step = min(tn, n)
        for c in range(0, n, step):
            o_ref[:, c:c + step] = jnp.dot(h, w_ref[:, c:c + step],
                                           preferred_element_type=F32).astype(o_ref.dtype)


def _in_proj(x2, g, whg, wfox, wmq, wgate, wff, tm):
    n_rows = x2.shape[0]
    ws = (whg, wfox, wmq, wgate, wff)
    out_dtypes = (BF16, BF16, BF16, BF16, F32)
    row = lambda i: (i, 0)
    return pl.pallas_call(
        functools.partial(_inproj_kernel, tn=512),
        grid=(n_rows // tm,),
        in_specs=[pl.BlockSpec((tm, D_MODEL), row), _const_spec(g.shape)]
                 + [_const_spec(w.shape) for w in ws],
        out_specs=[pl.BlockSpec((tm, w.shape[1]), row) for w in ws],
        out_shape=[jax.ShapeDtypeStruct((n_rows, w.shape[1]), dt) for w, dt in zip(ws, out_dtypes)],
        compiler_params=_params(("parallel",)),
        name="in_proj",
    )(x2, g, *ws)


def _hgrn_kernel(q_ref, f_ref, i_ref, go_ref, lbl_ref, ng_ref, o_ref, qs_ref, ks_ref, g_ref, *, seq):
    lbl = lbl_ref[...]
    e = jnp.exp(lbl - jnp.max(lbl, axis=0, keepdims=True))
    lb = e[0:1] / jnp.sum(e, axis=0, keepdims=True)

    f = lb + (1.0 - lb) * _sigmoid(f_ref[...].astype(F32))
    qs_ref[...] = _silu(q_ref[...].astype(F32))
    ks_ref[...] = 1.0 - f
    g = jnp.log(f)
    pos = lax.broadcasted_iota(jnp.int32, (seq, HG_D), 0) % CHUNK
    sh = 1
    while sh < CHUNK:
        g = g + jnp.where(pos >= sh, pltpu.roll(g, sh, axis=0), 0.0)
        sh *= 2
    g_ref[...] = g

    row16 = lax.broadcasted_iota(jnp.int32, (SUB, CHUNK), 0)
    col64 = lax.broadcasted_iota(jnp.int32, (SUB, CHUNK), 1)
    krow = lax.broadcasted_iota(jnp.int32, (CHUNK, HG_D), 0)
    ng = ng_ref[...]

    def chunk(c, st):
        r0 = pl.multiple_of(c * CHUNK, CHUNK)
        rows = pl.ds(r0, CHUNK)
        qc = qs_ref[rows, :]
        kc = ks_ref[rows, :]
        gc = g_ref[rows, :]
        ic = i_ref[rows, :]
        q_in = (qc * jnp.exp(gc)).astype(BF16)
        o = lax.dot_general(q_in, st.astype(BF16), (((1,), (1,)), ((), ())),
                            preferred_element_type=F32)
        blocks = []
        for bi in range(CHUNK // SUB):
            lo = bi * SUB
            gi = gc[lo:lo + SUB]
            qi = qc[lo:lo + SUB]
            ki = kc[lo:lo + SUB]
            a_blk = jnp.zeros((SUB, CHUNK), F32)
            for s in range(SUB):
                d = jnp.exp(jnp.minimum(gi - gi[s:s + 1], 0.0))
                col = jnp.sum(qi * ki[s:s + 1] * d, axis=-1, keepdims=True)
                a_blk = jnp.where(col64 == lo + s, col, a_blk)
            a_blk = jnp.where(row16 + lo >= col64, a_blk, 0.0)
            if bi > 0:
                piv = gc[lo - 1:lo]
                q_t = (qi * jnp.exp(gi - piv)).astype(BF16)
                k_t = jnp.where(krow < lo, kc * jnp.exp(jnp.minimum(piv - gc, 0.0)), 0.0).astype(BF16)
                a_blk = a_blk + lax.dot_general(q_t, k_t, (((1,), (1,)), ((), ())),
                                                preferred_element_type=F32)
            blocks.append(a_blk)
        a = jnp.concatenate(blocks, axis=0).astype(BF16)
        o = o + jnp.dot(a, ic, preferred_element_type=F32)
        g_last = gc[CHUNK - 1:CHUNK]
        k_up = (kc * jnp.exp(g_last - gc)).astype(BF16)
        i_t = ic.astype(F32).T.astype(BF16)
        st = st * jnp.exp(g_last) + jnp.dot(i_t, k_up, preferred_element_type=F32)
        y = _rms_rows(o, ng) * _silu(go_ref[rows, :].astype(F32))
        o_ref[rows, :] = y.astype(o_ref.dtype)
        return st

    lax.fori_loop(0, seq // CHUNK, chunk, jnp.zeros((HG_D, HG_D), F32))


def _hgrn(zhg, lb_logits, norm_g, batch, seq):
    zhg3 = zhg.reshape(batch, seq, 4 * HG_WIDTH)
    blk = lambda off: pl.BlockSpec((None, seq, HG_D), lambda b, h: (b, 0, off * HG_HEADS + h))
    n_lb = lb_logits.shape[0]
    return pl.pallas_call(
        functools.partial(_hgrn_kernel, seq=seq),
        grid=(batch, HG_HEADS),
        in_specs=[blk(0), blk(1), blk(2), blk(3),
                  pl.BlockSpec((n_lb, HG_D), lambda b, h: (0, h)),
                  pl.BlockSpec((1, HG_D), lambda b, h: (0, 0))],
        out_specs=pl.BlockSpec((None, seq, HG_D), lambda b, h: (b, 0, h)),
        out_shape=jax.ShapeDtypeStruct((batch, seq, HG_WIDTH), BF16),
        scratch_shapes=[pltpu.VMEM((seq, HG_D), F32)] * 3,
        compiler_params=_params(("parallel", "parallel")),
        name="hgrn",
    )(zhg3, zhg3, zhg3, zhg3, lb_logits, norm_g)


FOX_AUG = FOX_DH


def _split3(x):
    hi = x.astype(BF16).astype(F32)
    r = x - hi
    lo = r.astype(BF16).astype(F32)
    return hi, lo, r - lo


def _foxprep_kernel(q_ref, k_ref, ff_ref, fb_ref, qg_ref, kg_ref, bd_ref, qa_ref, ka_ref, *, seq):
    x = ff_ref[...] + fb_ref[...]
    fc = -(jnp.maximum(-x, 0.0) + jnp.log1p(jnp.exp(-jnp.abs(x))))
    pos = lax.broadcasted_iota(jnp.int32, (seq, LANES), 0)
    sh = 1
    while sh < seq:
        fc = fc + jnp.where(pos >= sh, pltpu.roll(fc, sh, axis=0), 0.0)
        sh *= 2

    def normed(ref, g_ref, scale):
        v = ref[...].astype(F32)
        ms = jnp.dot((v * v).astype(BF16), bd_ref[...], preferred_element_type=F32)
        return v * lax.rsqrt(ms + EPS) * (g_ref[...] * scale)

    qn = normed(q_ref, qg_ref, FOX_DH ** -0.5)
    kn = normed(k_ref, kg_ref, 1.0)
    lane = lax.broadcasted_iota(jnp.int32, (seq, LANES), 1)
    one = jnp.ones((seq, LANES), F32)
    zero = jnp.zeros((seq, LANES), F32)
    for h in range(FOX_HEADS):
        hi, lo, lo2 = _split3(jnp.broadcast_to(fc[:, h:h + 1], (seq, LANES)))
        a, b, c = FOX_AUG, FOX_AUG + 1, FOX_AUG + 2
        is_ones_q = (lane >= FOX_AUG + 3) & (lane < FOX_AUG + 6)
        is_ones_k = (lane >= FOX_AUG) & (lane < FOX_AUG + 3)
        aug_q = jnp.where(lane == a, hi, jnp.where(lane == b, lo, jnp.where(lane == c, lo2,
                          jnp.where(is_ones_q, one, zero))))
        aug_k = jnp.where(lane == a + 3, -hi, jnp.where(lane == b + 3, -lo, jnp.where(lane == c + 3, -lo2,
                          jnp.where(is_ones_k, one, zero))))
        pair = slice(LANES * (h // 2), LANES * (h // 2) + LANES)
        pq, pk = qn[:, pair], kn[:, pair]
        if h % 2:
            pq = pltpu.roll(pq, FOX_DH, axis=1)
            pk = pltpu.roll(pk, FOX_DH, axis=1)
        qa_ref[h] = jnp.where(lane < FOX_DH, pq, aug_q).astype(BF16)
        ka_ref[h] = jnp.where(lane < FOX_DH, pk, aug_k).astype(BF16)


def _fox_prep(zfox3, zff3, fb, qg, kg, bd, batch, seq):
    head_out = pl.BlockSpec((None, FOX_HEADS, seq, LANES), lambda b: (b, 0, 0, 0))
    return pl.pallas_call(
        functools.partial(_foxprep_kernel, seq=seq),
        grid=(batch,),
        in_specs=[pl.BlockSpec((None, seq, FOX_WIDTH), lambda b: (b, 0, 0)),
                  pl.BlockSpec((None, seq, FOX_WIDTH), lambda b: (b, 0, 1)),
                  pl.BlockSpec((None, seq, LANES), lambda b: (b, 0, 0)),
                  _const_spec(fb.shape), _const_spec(qg.shape), _const_spec(kg.shape), _const_spec(bd.shape)],
        out_specs=[head_out, head_out],
        out_shape=[jax.ShapeDtypeStruct((batch, FOX_HEADS, seq, LANES), BF16)] * 2,
        compiler_params=_params(("parallel",)),
        name="fox_prep",
    )(zfox3, zfox3, zff3, fb, qg, kg, bd)


def _foxattn_kernel(qa_ref, ka_ref, v_ref, o_ref, *, tq):
    i = pl.program_id(2)
    lane = lax.broadcasted_iota(jnp.int32, (tq, LANES), 1)
    rowi = lax.broadcasted_iota(jnp.int32, (tq, tq), 0)
    coli = lax.broadcasted_iota(jnp.int32, (tq, tq), 1)
    outs = []
    for hh in range(2):
        q = qa_ref[hh]

        def scores(j):
            ks = ka_ref[hh, pl.ds(pl.multiple_of(j * tq, tq), tq), :]
            return lax.dot_general(q, ks, (((1,), (1,)), ((), ())), preferred_element_type=F32)

        def update(carry, s, j):
            m, l, acc = carry
            m_new = jnp.maximum(m, jnp.max(s, axis=-1, keepdims=True))
            alpha = jnp.exp(m - m_new)
            p = jnp.exp(s - m_new)
            vs = v_ref[pl.ds(pl.multiple_of(j * tq, tq), tq), :]
            acc = alpha * acc + jnp.dot(p.astype(BF16), vs, preferred_element_type=F32)
            return m_new, alpha * l + jnp.sum(p, axis=-1, keepdims=True), acc

        init = (jnp.full((tq, 1), NEG_BIG, F32), jnp.zeros((tq, 1), F32), jnp.zeros((tq, LANES), F32))
        carry = lax.fori_loop(0, i, lambda j, c: update(c, scores(j), j), init)
        s = jnp.where(rowi >= coli, scores(i), NEG_BIG)
        _, l, acc = update(carry, s, i)
        outs.append(acc / l)
    o_ref[...] = jnp.where(lane < FOX_DH, outs[0], outs[1]).astype(o_ref.dtype)


def _fox_attn(qa, ka, zfox3, batch, seq, tq):
    v_col0 = 2 * FOX_WIDTH // LANES
    return pl.pallas_call(
        functools.partial(_foxattn_kernel, tq=tq),
        grid=(batch, FOX_HEADS // 2, seq // tq),
        in_specs=[pl.BlockSpec((None, 2, tq, LANES), lambda b, p, i: (b, p, i, 0)),
                  pl.BlockSpec((None, 2, seq, LANES), lambda b, p, i: (b, p, 0, 0)),
                  pl.BlockSpec((None, seq, LANES), lambda b, p, i: (b, 0, v_col0 + p))],
        out_specs=pl.BlockSpec((None, tq, LANES), lambda b, p, i: (b, i, p)),
        out_shape=jax.ShapeDtypeStruct((batch, seq, FOX_WIDTH), BF16),
        compiler_params=_params(("parallel", "parallel", "arbitrary")),
        name="fox_attn",
    )(qa, ka, zfox3)


def _memkv_kernel(m_ref, g_ref, w_ref, kg_ref, k_ref, v_ref):
    h = _rms_rows(m_ref[...], g_ref[...]).astype(BF16)
    kv = jnp.dot(h, w_ref[...], preferred_element_type=F32)
    for hd in range(MEM_HEADS):
        sl = slice(hd * MEM_DH, (hd + 1) * MEM_DH)
        k_ref[:, sl] = _rms_rows(kv[:, sl], kg_ref[...]).astype(k_ref.dtype)
    v_ref[...] = kv[:, MEM_WIDTH:].astype(v_ref.dtype)


def _mem_kv(mem2, g, w, kg, tm):
    n_rows = mem2.shape[0]
    row = lambda i: (i, 0)
    return pl.pallas_call(
        _memkv_kernel,
        grid=(n_rows // tm,),
        in_specs=[pl.BlockSpec((tm, D_MODEL), row), _const_spec(g.shape), _const_spec(w.shape),
                  _const_spec(kg.shape)],
        out_specs=[pl.BlockSpec((tm, MEM_WIDTH), row)] * 2,
        out_shape=[jax.ShapeDtypeStruct((n_rows, MEM_WIDTH), BF16)] * 2,
        compiler_params=_params(("parallel",)),
        name="mem_kv",
    )(mem2, g, w, kg)


def _memattn_kernel(q_ref, k_ref, v_ref, qg_ref, o_ref):
    for hd in range(MEM_HEADS):
        sl = slice(hd * MEM_DH, (hd + 1) * MEM_DH)
        q = _rms_rows(q_ref[:, sl].astype(F32), qg_ref[...] * (MEM_DH ** -0.5)).astype(BF16)
        s = lax.dot_general(q, k_ref[:, sl], (((1,), (1,)), ((), ())), preferred_element_type=F32)
        p = jnp.exp(s - jnp.max(s, axis=-1, keepdims=True))
        l = jnp.sum(p, axis=-1, keepdims=True)
        o = jnp.dot(p.astype(BF16), v_ref[:, sl], preferred_element_type=F32)
        o_ref[:, sl] = (o / l).astype(o_ref.dtype)


def _mem_attn(zmq3, mk3, mv3, qg, batch, seq, tq):
    n_mem = mk3.shape[1]
    kv_spec = pl.BlockSpec((None, n_mem, MEM_WIDTH), lambda b, i: (b, 0, 0))
    return pl.pallas_call(
        _memattn_kernel,
        grid=(batch, seq // tq),
        in_specs=[pl.BlockSpec((None, tq, MEM_WIDTH), lambda b, i: (b, i, 0)), kv_spec, kv_spec,
                  _const_spec(qg.shape)],
        out_specs=pl.BlockSpec((None, tq, MEM_WIDTH), lambda b, i: (b, i, 0)),
        out_shape=jax.ShapeDtypeStruct((batch, seq, MEM_WIDTH), BF16),
        compiler_params=_params(("parallel", "parallel")),
        name="mem_attn",
    )(zmq3, mk3, mv3, qg)


def _merge_kernel(x_ref, ya_ref, yb_ref, yc_ref, gt_ref, wa_ref, wb_ref, wc_ref, wo_ref, ng_ref,
                  x1_ref, h2_ref):
    merged = None
    for br, (y_ref, w_ref) in enumerate(((ya_ref, wa_ref), (yb_ref, wb_ref), (yc_ref, wc_ref))):
        gate = _sigmoid(gt_ref[:, br * D_MODEL:(br + 1) * D_MODEL].astype(F32))
        t = gate * jnp.dot(y_ref[...], w_ref[...], preferred_element_type=F32)
        merged = t if merged is None else merged + t
    x1 = x_ref[...] + jnp.dot(merged.astype(BF16), wo_ref[...], preferred_element_type=F32)
    x1_ref[...] = x1
    h2_ref[...] = _rms_rows(x1, ng_ref[...]).astype(h2_ref.dtype)


def _merge(x2, ya, yb, yc, zgate, wa, wb, wc, wo, ng, tm):
    n_rows = x2.shape[0]
    row = lambda i: (i, 0)
    rows = lambda w: pl.BlockSpec((tm, w), row)
    return pl.pallas_call(
        _merge_kernel,
        grid=(n_rows // tm,),
        in_specs=[rows(D_MODEL), rows(HG_WIDTH), rows(FOX_WIDTH), rows(MEM_WIDTH), rows(N_BRANCH * D_MODEL),
                  _const_spec(wa.shape), _const_spec(wb.shape), _const_spec(wc.shape), _const_spec(wo.shape),
                  _const_spec(ng.shape)],
        out_specs=[rows(D_MODEL), rows(D_MODEL)],
        out_shape=[jax.ShapeDtypeStruct((n_rows, D_MODEL), F32), jax.ShapeDtypeStruct((n_rows, D_MODEL), BF16)],
        compiler_params=_params(("parallel",)),
        name="merge",
    )(x2, ya, yb, yc, zgate, wa, wb, wc, wo, ng)


def _ffn_kernel(h_ref, x1_ref, wa_ref, wv_ref, cw_ref, cb_ref, wd_ref, o_ref, carry_ref, y_ref, *, tm, fc):
    @pl.when(pl.program_id(1) == 0)
    def _():
        carry_ref[...] = jnp.zeros_like(carry_ref)

    h = h_ref[...]
    row = lax.broadcasted_iota(jnp.int32, (tm, fc), 0)
    for c in range(D_FF // fc):
        sl = slice(c * fc, (c + 1) * fc)
        a = jnp.dot(h, wa_ref[:, sl], preferred_element_type=F32)
        v = jnp.dot(h, wv_ref[:, sl], preferred_element_type=F32)
        prev = carry_ref[c]
        p1 = prev[SUBLANES - 1:SUBLANES]
        p2 = prev[SUBLANES - 2:SUBLANES - 1]
        a1 = jnp.where(row == 0, p1, pltpu.roll(a, 1, axis=0))
        a2 = jnp.where(row == 0, p2, jnp.where(row == 1, p1, pltpu.roll(a, 2, axis=0)))
        carry_ref[c] = a[tm - SUBLANES:tm]
        z = a2 * cw_ref[0:1, sl] + a1 * cw_ref[1:2, sl] + a * cw_ref[2:3, sl] + cb_ref[:, sl]
        gelu = 0.5 * z * (1.0 + lax.erf(z * (2.0 ** -0.5)))
        y_ref[:, sl] = (gelu * v).astype(y_ref.dtype)
    o_ref[...] = x1_ref[...] + jnp.dot(y_ref[...], wd_ref[...], preferred_element_type=F32)


def _ffn(h2, x1, wa, wv, cw, cb, wd, batch, seq, tm, fc):
    h3 = h2.reshape(batch, seq, D_MODEL)
    x3 = x1.reshape(batch, seq, D_MODEL)
    tile = pl.BlockSpec((None, tm, D_MODEL), lambda b, t: (b, t, 0))
    return pl.pallas_call(
        functools.partial(_ffn_kernel, tm=tm, fc=fc),
        grid=(batch, seq // tm),
        in_specs=[tile, tile, _const_spec(wa.shape), _const_spec(wv.shape), _const_spec(cw.shape),
                  _const_spec(cb.shape), _const_spec(wd.shape)],
        out_specs=tile,
        out_shape=jax.ShapeDtypeStruct((batch, seq, D_MODEL), F32),
        scratch_shapes=[pltpu.VMEM((D_FF // fc, SUBLANES, fc), F32), pltpu.VMEM((tm, D_FF), BF16)],
        compiler_params=_params(("parallel", "arbitrary")),
        name="ffn",
    )(h3, x3, wa, wv, cw, cb, wd)


def kernel(x, mem, norm_mix_g, norm_mem_g, w_in, hgrn_lb_logits, hgrn_norm_g, fox_f_bias, fox_q_norm_g,
           fox_k_norm_g, mem_kv_w, mem_q_norm_g, mem_k_norm_g, w_br_hgrn, w_br_fox, w_br_mem, w_out,
           norm_ffn_g, ffn_w_up, ffn_conv_w, ffn_conv_b, ffn_w_down):
    batch, seq, _ = x.shape
    n_mem = mem.shape[1]
    depth = w_in.shape[0]
    assert depth == 1 and seq % 512 == 0

    x2 = x.reshape(batch * seq, D_MODEL)
    for l in range(depth):
        w = w_in[l]
        c_hg, c_fox = 4 * HG_WIDTH, 3 * FOX_WIDTH
        o_ff = c_hg + c_fox
        o_mq = o_ff + FOX_HEADS
        o_gate = o_mq + MEM_WIDTH
        whg = w[:, :c_hg].astype(BF16)
        wfox = w[:, c_hg:o_ff].astype(BF16)
        wff = jnp.pad(w[:, o_ff:o_mq], ((0, 0), (0, LANES - FOX_HEADS))).astype(BF16)
        wmq = w[:, o_mq:o_gate].astype(BF16)
        wgate = w[:, o_gate:].astype(BF16)
        row = lambda v: v.reshape(1, -1).astype(F32)

        zhg, zfox, zmq, zgate, zff = _in_proj(x2, row(norm_mix_g[l]), whg, wfox, wmq, wgate, wff, tm=512)

        ya = _hgrn(zhg, hgrn_lb_logits.astype(F32), row(hgrn_norm_g[l]), batch, seq)

        zfox3 = zfox.reshape(batch, seq, c_fox)
        fb = jnp.pad(row(fox_f_bias[l]), ((0, 0), (0, LANES - FOX_HEADS)))
        head_id = jnp.arange(FOX_WIDTH) // FOX_DH
        bd = (head_id[:, None] == head_id[None, :]).astype(BF16) * (1.0 / FOX_DH)
        qa, ka = _fox_prep(zfox3, zff.reshape(batch, seq, LANES), fb,
                           jnp.tile(row(fox_q_norm_g[l]), (1, FOX_HEADS)),
                           jnp.tile(row(fox_k_norm_g[l]), (1, FOX_HEADS)), bd, batch, seq)
        yb = _fox_attn(qa, ka, zfox3, batch, seq, tq=256)

        mk, mv = _mem_kv(mem.reshape(batch * n_mem, D_MODEL), row(norm_mem_g[l]), mem_kv_w[l].astype(BF16),
                         row(mem_k_norm_g[l]), tm=512)
        yc = _mem_attn(zmq.reshape(batch, seq, MEM_WIDTH), mk.reshape(batch, n_mem, MEM_WIDTH),
                       mv.reshape(batch, n_mem, MEM_WIDTH), row(mem_q_norm_g[l]), batch, seq, tq=512)

        x1, h2 = _merge(x2, ya.reshape(batch * seq, HG_WIDTH), yb.reshape(batch * seq, FOX_WIDTH),
                        yc.reshape(batch * seq, MEM_WIDTH), zgate, w_br_hgrn[l].astype(BF16),
                        w_br_fox[l].astype(BF16), w_br_mem[l].astype(BF16), w_out[l].astype(BF16),
                        row(norm_ffn_g[l]), tm=512)

        wup = ffn_w_up[l]
        out = _ffn(h2, x1, wup[:, :D_FF].astype(BF16), wup[:, D_FF:].astype(BF16), ffn_conv_w[l].astype(F32),
                   row(ffn_conv_b[l]), ffn_w_down[l].astype(BF16), batch, seq, tm=512, fc=256)
        x2 = out.reshape(batch * seq, D_MODEL)
    return x2.reshape(batch, seq, D_MODEL)
```

```python
import functools

import jax
import jax.numpy as jnp
from jax import lax
from jax.experimental import pallas as pl
from jax.experimental.pallas import tpu as pltpu

F32 = jnp.float32
BF16 = jnp.bfloat16

D_MODEL = 1024
EPS = 1e-6
CHUNK = 64
SUB = 16
HG_HEADS = 4
HG_D = 128
HG_WIDTH = HG_HEADS * HG_D
FOX_HEADS = 8
FOX_DH = 64
FOX_WIDTH = FOX_HEADS * FOX_DH
MEM_HEADS = 4
MEM_DH = 128
MEM_WIDTH = MEM_HEADS * MEM_DH
N_BRANCH = 3
D_FF = 2816
CONV_W = 3

LANES = 128
SUBLANES = 8
VMEM_LIMIT = 56 * 1024 * 1024

NEG_BIG = -1e30


def _const_spec(shape):
    nd = len(shape)
    return pl.BlockSpec(shape, lambda *_: (0,) * nd, pipeline_mode=pl.Buffered(1))


def _params(sem):
    return pltpu.CompilerParams(dimension_semantics=sem, vmem_limit_bytes=VMEM_LIMIT)


def _rms_rows(x, g):
    return x * lax.rsqrt(jnp.mean(x * x, axis=-1, keepdims=True) + EPS) * g


def _sigmoid(x):
    return 1.0 / (1.0 + jnp.exp(-x))


def _silu(x):
    return x * _sigmoid(x)


def _inproj_kernel(x_ref, g_ref, whg_ref, wfox_ref, wmq_ref, wgate_ref, wff_ref,
                   zhg_ref, zfox_ref, zmq_ref, zgate_ref, zff_ref, *, tn):
    h = _rms_rows(x_ref[...], g_ref[...]).astype(BF16)
    for w_ref, o_ref in ((whg_ref, zhg_ref), (wfox_ref, zfox_ref), (wmq_ref, zmq_ref),
                         (wgate_ref, zgate_ref), (wff_ref, zff_ref)):
        n = w_ref.shape[1]
        step = min(tn, n)
        for c in range(0, n, step):
            o_ref[:, c:c + step] = jnp.dot(h, w_ref[:, c:c + step],
                                           preferred_element_type=F32).astype(o_ref.dtype)


def _in_proj(x2, g, whg, wfox, wmq, wgate, wff, tm):
    n_rows = x2.shape[0]
    ws = (whg, wfox, wmq, wgate, wff)
    out_dtypes = (BF16, BF16, BF16, BF16, F32)
    row = lambda i: (i, 0)
    return pl.pallas_call(
        functools.partial(_inproj_kernel, tn=512),
        grid=(n_rows // tm,),
        in_specs=[pl.BlockSpec((tm, D_MODEL), row), _const_spec(g.shape)]
                 + [_const_spec(w.shape) for w in ws],
        out_specs=[pl.BlockSpec((tm, w.shape[1]), row) for w in ws],
        out_shape=[jax.ShapeDtypeStruct((n_rows, w.shape[1]), dt) for w, dt in zip(ws, out_dtypes)],
        compiler_params=_params(("parallel",)),
        name="in_proj",
    )(x2, g, *ws)


HG_LEVELS = (32, 16, 8)
HG_DIAG = 8
LOG2E = 1.4426950408889634


def _hgrn_kernel(q_ref, f_ref, i_ref, go_ref, lbl_ref, ng_ref, o_ref, st_ref, *, seq):
    lbl = lbl_ref[...]
    e = jnp.exp(lbl - jnp.max(lbl, axis=0, keepdims=True))
    lb = e[0:1] / jnp.sum(e, axis=0, keepdims=True)
    ng = ng_ref[...]
    st_ref[...] = jnp.zeros_like(st_ref)

    pos = lax.broadcasted_iota(jnp.int32, (CHUNK, HG_WIDTH), 0)
    row = lax.broadcasted_iota(jnp.int32, (CHUNK, HG_D), 0)
    row_a = lax.broadcasted_iota(jnp.int32, (CHUNK, CHUNK), 0)
    col_a = lax.broadcasted_iota(jnp.int32, (CHUNK, CHUNK), 1)
    lane8 = lax.broadcasted_iota(jnp.int32, (HG_DIAG, HG_D), 1)
    is_q = {h: (row & h) != 0 for h in HG_LEVELS}
    same_blk = {h: (row_a // (2 * h)) == (col_a // (2 * h)) for h in HG_LEVELS}
    diag_mask = ((row_a // HG_DIAG) == (col_a // HG_DIAG)) & (row_a >= col_a)
    nt = (((1,), (1,)), ((), ()))

    def chunk(c, carry):
        rows = pl.ds(pl.multiple_of(c * CHUNK, CHUNK), CHUNK)
        f = lb + (1.0 - lb) * _sigmoid(f_ref[rows, :].astype(F32))
        k_all = 1.0 - f
        g_all = jnp.log(f) * LOG2E
        sh = 1
        while sh < CHUNK:
            g_all = g_all + jnp.where(pos >= sh, pltpu.roll(g_all, sh, axis=0), 0.0)
            sh *= 2
        q_all = _silu(q_ref[rows, :].astype(F32))
        go_all = _silu(go_ref[rows, :].astype(F32))
        for hd in range(HG_HEADS):
            sl = slice(hd * HG_D, (hd + 1) * HG_D)
            qc, kc, gc = q_all[:, sl], k_all[:, sl], g_all[:, sl]
            ic = i_ref[rows, sl]
            st = st_ref[hd]
            q_in = (qc * jnp.exp2(gc)).astype(BF16)
            o = lax.dot_general(q_in, st.astype(BF16), nt, preferred_element_type=F32)
            a = jnp.zeros((CHUNK, CHUNK), F32)
            for h in HG_LEVELS:
                piv = [jnp.broadcast_to(gc[p:p + 1], (2 * h, HG_D)) for p in range(h - 1, CHUNK, 2 * h)]
                piv = piv[0] if len(piv) == 1 else jnp.concatenate(piv, axis=0)
                d = jnp.exp2(-jnp.abs(gc - piv))
                q_t = jnp.where(is_q[h], qc * d, 0.0).astype(BF16)
                k_t = jnp.where(is_q[h], 0.0, kc * d).astype(BF16)
                a = jnp.where(same_blk[h], lax.dot_general(q_t, k_t, nt, preferred_element_type=F32), a)
            blocks = []
            for b in range(CHUNK // HG_DIAG):
                lo = b * HG_DIAG
                g8, q8, k8 = gc[lo:lo + HG_DIAG], qc[lo:lo + HG_DIAG], kc[lo:lo + HG_DIAG]
                a8 = jnp.zeros((HG_DIAG, HG_D), F32)
                for s in range(HG_DIAG):
                    d = jnp.exp2(jnp.minimum(g8 - g8[s:s + 1], 0.0))
                    col = jnp.sum(q8 * k8[s:s + 1] * d, axis=-1, keepdims=True)
                    a8 = jnp.where(lane8 == s, col, a8)
                blocks.append(pltpu.roll(a8, lo, axis=1) if lo else a8)
            a_diag = jnp.concatenate(blocks, axis=0)[:, :CHUNK]
            a = jnp.where(diag_mask, a_diag, a).astype(BF16)
            o = o + jnp.dot(a, ic, preferred_element_type=F32)
            g_last = gc[CHUNK - 1:CHUNK]
            k_up = (kc * jnp.exp2(g_last - gc)).astype(BF16)
            i_t = ic.astype(F32).T.astype(BF16)
            st_ref[hd] = st * jnp.exp2(g_last) + jnp.dot(i_t, k_up, preferred_element_type=F32)
            o_ref[rows, sl] = (_rms_rows(o, ng) * go_all[:, sl]).astype(o_ref.dtype)
        return carry

    lax.fori_loop(0, seq // CHUNK, chunk, 0)


def _hgrn(zhg, lb_logits, norm_g, batch, seq):
    zhg3 = zhg.reshape(batch, seq, 4 * HG_WIDTH)
    blk = lambda off: pl.BlockSpec((None, seq, HG_WIDTH), lambda b: (b, 0, off))
    return pl.pallas_call(
        functools.partial(_hgrn_kernel, seq=seq),
        grid=(batch,),
        in_specs=[blk(0), blk(1), blk(2), blk(3), _const_spec(lb_logits.shape), _const_spec(norm_g.shape)],
        out_specs=pl.BlockSpec((None, seq, HG_WIDTH), lambda b: (b, 0, 0)),
        out_shape=jax.ShapeDtypeStruct((batch, seq, HG_WIDTH), BF16),
        scratch_shapes=[pltpu.VMEM((HG_HEADS, HG_D, HG_D), F32)],
        compiler_params=_params(("parallel",)),
        name="hgrn",
    )(zhg3, zhg3, zhg3, zhg3, lb_logits, norm_g)


FOX_AUG = FOX_DH


def _split3(x):
    hi = x.astype(BF16).astype(F32)
    r = x - hi
    lo = r.astype(BF16).astype(F32)
    return hi, lo, r - lo


def _foxprep_kernel(q_ref, k_ref, ff_ref, fb_ref, qg_ref, kg_ref, bd_ref, qa_ref, ka_ref, *, seq):
    x = ff_ref[...] + fb_ref[...]
    fc = -(jnp.maximum(-x, 0.0) + jnp.log1p(jnp.exp(-jnp.abs(x))))
    pos = lax.broadcasted_iota(jnp.int32, (seq, LANES), 0)
    sh = 1
    while sh < seq:
        fc = fc + jnp.where(pos >= sh, pltpu.roll(fc, sh, axis=0), 0.0)
        sh *= 2
    fc = fc * LOG2E

    def normed(ref, g_ref, scale):
        v = ref[...].astype(F32)
        ms = jnp.dot((v * v).astype(BF16), bd_ref[...], preferred_element_type=F32)
        return v * lax.rsqrt(ms + EPS) * (g_ref[...] * scale)

    qn = normed(q_ref, qg_ref, FOX_DH ** -0.5 * LOG2E)
    kn = normed(k_ref, kg_ref, 1.0)
    lane = lax.broadcasted_iota(jnp.int32, (seq, LANES), 1)
    one = jnp.ones((seq, LANES), F32)
    zero = jnp.zeros((seq, LANES), F32)
    for h in range(FOX_HEADS):
        hi, lo, lo2 = _split3(jnp.broadcast_to(fc[:, h:h + 1], (seq, LANES)))
        a, b, c = FOX_AUG, FOX_AUG + 1, FOX_AUG + 2
        is_ones_q = (lane >= FOX_AUG + 3) & (lane < FOX_AUG + 6)
        is_ones_k = (lane >= FOX_AUG) & (lane < FOX_AUG + 3)
        aug_q = jnp.where(lane == a, hi, jnp.where(lane == b, lo, jnp.where(lane == c, lo2,
                          jnp.where(is_ones_q, one, zero))))
        aug_k = jnp.where(lane == a + 3, -hi, jnp.where(lane == b + 3, -lo, jnp.where(lane == c + 3, -lo2,
                          jnp.where(is_ones_k, one, zero))))
        pair = slice(LANES * (h // 2), LANES * (h // 2) + LANES)
        pq, pk = qn[:, pair], kn[:, pair]
        if h % 2:
            pq = pltpu.roll(pq, FOX_DH, axis=1)
            pk = pltpu.roll(pk, FOX_DH, axis=1)
        qa_ref[h] = jnp.where(lane < FOX_DH, pq, aug_q).astype(BF16)
        ka_ref[h] = jnp.where(lane < FOX_DH, pk, aug_k).astype(BF16)


def _fox_prep(zfox3, zff3, fb, qg, kg, bd, batch, seq):
    head_out = pl.BlockSpec((None, FOX_HEADS, seq, LANES), lambda b: (b, 0, 0, 0))
    return pl.pallas_call(
        functools.partial(_foxprep_kernel, seq=seq),
        grid=(batch,),
        in_specs=[pl.BlockSpec((None, seq, FOX_WIDTH), lambda b: (b, 0, 0)),
                  pl.BlockSpec((None, seq, FOX_WIDTH), lambda b: (b, 0, 1)),
                  pl.BlockSpec((None, seq, LANES), lambda b: (b, 0, 0)),
                  _const_spec(fb.shape), _const_spec(qg.shape), _const_spec(kg.shape), _const_spec(bd.shape)],
        out_specs=[head_out, head_out],
        out_shape=[jax.ShapeDtypeStruct((batch, FOX_HEADS, seq, LANES), BF16)] * 2,
        compiler_params=_params(("parallel",)),
        name="fox_prep",
    )(zfox3, zfox3, zff3, fb, qg, kg, bd)


def _foxattn_kernel(qa_ref, ka_ref, v_ref, o_ref, *, tq):
    i = pl.program_id(2)
    lane = lax.broadcasted_iota(jnp.int32, (tq, LANES), 1)
    rowi = lax.broadcasted_iota(jnp.int32, (tq, tq), 0)
    coli = lax.broadcasted_iota(jnp.int32, (tq, tq), 1)

    def step(j, carry, masked):
        kv_rows = pl.ds(pl.multiple_of(j * tq, tq), tq)
        vs = v_ref[kv_rows, :]
        new = []
        for hh in range(2):
            m, l, acc = carry[hh]
            s = lax.dot_general(qa_ref[hh], ka_ref[hh, kv_rows, :], (((1,), (1,)), ((), ())),
                                preferred_element_type=F32)
            if masked:
                s = jnp.where(rowi >= coli, s, NEG_BIG)
            m_new = jnp.maximum(m, jnp.max(s, axis=-1, keepdims=True))
            alpha = jnp.exp2(m - m_new)
            p = jnp.exp2(s - m_new)
            p_sum = p[:, :LANES]
            for k in range(1, tq // LANES):
                p_sum = p_sum + p[:, k * LANES:(k + 1) * LANES]
            acc = alpha * acc + jnp.dot(p.astype(BF16), vs, preferred_element_type=F32)
            new.append((m_new, alpha * l + p_sum, acc))
        return tuple(new)

    head_init = (jnp.full((tq, 1), NEG_BIG, F32), jnp.zeros((tq, LANES), F32), jnp.zeros((tq, LANES), F32))
    carry = lax.fori_loop(0, i, lambda j, c: step(j, c, False), (head_init, head_init))
    carry = step(i, carry, True)
    outs = [acc / jnp.sum(l, axis=-1, keepdims=True) for _, l, acc in carry]
    o_ref[...] = jnp.where(lane < FOX_DH, outs[0], outs[1]).astype(o_ref.dtype)


def _fox_attn(qa, ka, zfox3, batch, seq, tq):
    v_col0 = 2 * FOX_WIDTH // LANES
    return pl.pallas_call(
        functools.partial(_foxattn_kernel, tq=tq),
        grid=(batch, FOX_HEADS // 2, seq // tq),
        in_specs=[pl.BlockSpec((None, 2, tq, LANES), lambda b, p, i: (b, p, i, 0)),
                  pl.BlockSpec((None, 2, seq, LANES), lambda b, p, i: (b, p, 0, 0)),
                  pl.BlockSpec((None, seq, LANES), lambda b, p, i: (b, 0, v_col0 + p))],
        out_specs=pl.BlockSpec((None, tq, LANES), lambda b, p, i: (b, i, p)),
        out_shape=jax.ShapeDtypeStruct((batch, seq, FOX_WIDTH), BF16),
        compiler_params=_params(("parallel", "parallel", "arbitrary")),
        name="fox_attn",
    )(qa, ka, zfox3)


def _memkv_kernel(m_ref, g_ref, w_ref, kg_ref, k_ref, v_ref):
    h = _rms_rows(m_ref[...], g_ref[...]).astype(BF16)
    kv = jnp.dot(h, w_ref[...], preferred_element_type=F32)
    for hd in range(MEM_HEADS):
        sl = slice(hd * MEM_DH, (hd + 1) * MEM_DH)
        k_ref[:, sl] = _rms_rows(kv[:, sl], kg_ref[...]).astype(k_ref.dtype)
    v_ref[...] = kv[:, MEM_WIDTH:].astype(v_ref.dtype)


def _mem_kv(mem2, g, w, kg, tm):
    n_rows = mem2.shape[0]
    row = lambda i: (i, 0)
    return pl.pallas_call(
        _memkv_kernel,
        grid=(n_rows // tm,),
        in_specs=[pl.BlockSpec((tm, D_MODEL), row), _const_spec(g.shape), _const_spec(w.shape),
                  _const_spec(kg.shape)],
        out_specs=[pl.BlockSpec((tm, MEM_WIDTH), row)] * 2,
        out_shape=[jax.ShapeDtypeStruct((n_rows, MEM_WIDTH), BF16)] * 2,
        compiler_params=_params(("parallel",)),
        name="mem_kv",
    )(mem2, g, w, kg)


def _memattn_kernel(q_ref, k_ref, v_ref, qg_ref, o_ref):
    for hd in range(MEM_HEADS):
        sl = slice(hd * MEM_DH, (hd + 1) * MEM_DH)
        q = _rms_rows(q_ref[:, sl].astype(F32), qg_ref[...] * (MEM_DH ** -0.5)).astype(BF16)
        s = lax.dot_general(q, k_ref[:, sl], (((1,), (1,)), ((), ())), preferred_element_type=F32)
        p = jnp.exp(s - jnp.max(s, axis=-1, keepdims=True))
        l = jnp.sum(p, axis=-1, keepdims=True)
        o = jnp.dot(p.astype(BF16), v_ref[:, sl], preferred_element_type=F32)
        o_ref[:, sl] = (o / l).astype(o_ref.dtype)


def _mem_attn(zmq3, mk3, mv3, qg, batch, seq, tq):
    n_mem = mk3.shape[1]
    kv_spec = pl.BlockSpec((None, n_mem, MEM_WIDTH), lambda b, i: (b, 0, 0))
    return pl.pallas_call(
        _memattn_kernel,
        grid=(batch, seq // tq),
        in_specs=[pl.BlockSpec((None, tq, MEM_WIDTH), lambda b, i: (b, i, 0)), kv_spec, kv_spec,
                  _const_spec(qg.shape)],
        out_specs=pl.BlockSpec((None, tq, MEM_WIDTH), lambda b, i: (b, i, 0)),
        out_shape=jax.ShapeDtypeStruct((batch, seq, MEM_WIDTH), BF16),
        compiler_params=_params(("parallel", "parallel")),
        name="mem_attn",
    )(zmq3, mk3, mv3, qg)


def _merge_kernel(x_ref, ya_ref, yb_ref, yc_ref, gt_ref, wa_ref, wb_ref, wc_ref, wo_ref, ng_ref,
                  x1_ref, h2_ref):
    merged = None
    for br, (y_ref, w_ref) in enumerate(((ya_ref, wa_ref), (yb_ref, wb_ref), (yc_ref, wc_ref))):
        gate = _sigmoid(gt_ref[:, br * D_MODEL:(br + 1) * D_MODEL].astype(F32))
        t = gate * jnp.dot(y_ref[...], w_ref[...], preferred_element_type=F32)
        merged = t if merged is None else merged + t
    x1 = x_ref[...] + jnp.dot(merged.astype(BF16), wo_ref[...], preferred_element_type=F32)
    x1_ref[...] = x1
    h2_ref[...] = _rms_rows(x1, ng_ref[...]).astype(h2_ref.dtype)


def _merge(x2, ya, yb, yc, zgate, wa, wb, wc, wo, ng, tm):
    n_rows = x2.shape[0]
    row = lambda i: (i, 0)
    rows = lambda w: pl.BlockSpec((tm, w), row)
    return pl.pallas_call(
        _merge_kernel,
        grid=(n_rows // tm,),
        in_specs=[rows(D_MODEL), rows(HG_WIDTH), rows(FOX_WIDTH), rows(MEM_WIDTH), rows(N_BRANCH * D_MODEL),
                  _const_spec(wa.shape), _const_spec(wb.shape), _const_spec(wc.shape), _const_spec(wo.shape),
                  _const_spec(ng.shape)],
        out_specs=[rows(D_MODEL), rows(D_MODEL)],
        out_shape=[jax.ShapeDtypeStruct((n_rows, D_MODEL), F32), jax.ShapeDtypeStruct((n_rows, D_MODEL), BF16)],
        compiler_params=_params(("parallel",)),
        name="merge",
    )(x2, ya, yb, yc, zgate, wa, wb, wc, wo, ng)


def _ffn_kernel(h_ref, x1_ref, wa_ref, wv_ref, cw_ref, cb_ref, wd_ref, o_ref, carry_ref, y_ref, *, tm, fc):
    @pl.when(pl.program_id(1) == 0)
    def _():
        carry_ref[...] = jnp.zeros_like(carry_ref)

    h = h_ref[...]
    row = lax.broadcasted_iota(jnp.int32, (tm, fc), 0)
    for c in range(D_FF // fc):
        sl = slice(c * fc, (c + 1) * fc)
        a = jnp.dot(h, wa_ref[:, sl], preferred_element_type=F32)
        v = jnp.dot(h, wv_ref[:, sl], preferred_element_type=F32)
        prev = carry_ref[c]
        p1 = prev[SUBLANES - 1:SUBLANES]
        p2 = prev[SUBLANES - 2:SUBLANES - 1]
        a1 = jnp.where(row == 0, p1, pltpu.roll(a, 1, axis=0))
        a2 = jnp.where(row == 0, p2, jnp.where(row == 1, p1, pltpu.roll(a, 2, axis=0)))
        carry_ref[c] = a[tm - SUBLANES:tm]
        z = a2 * cw_ref[0:1, sl] + a1 * cw_ref[1:2, sl] + a * cw_ref[2:3, sl] + cb_ref[:, sl]
        gelu = 0.5 * z * (1.0 + lax.erf(z * (2.0 ** -0.5)))
        y_ref[:, sl] = (gelu * v).astype(y_ref.dtype)
    o_ref[...] = x1_ref[...] + jnp.dot(y_ref[...], wd_ref[...], preferred_element_type=F32)


def _ffn(h2, x1, wa, wv, cw, cb, wd, batch, seq, tm, fc):
    h3 = h2.reshape(batch, seq, D_MODEL)
    x3 = x1.reshape(batch, seq, D_MODEL)
    tile = pl.BlockSpec((None, tm, D_MODEL), lambda b, t: (b, t, 0))
    return pl.pallas_call(
        functools.partial(_ffn_kernel, tm=tm, fc=fc),
        grid=(batch, seq // tm),
        in_specs=[tile, tile, _const_spec(wa.shape), _const_spec(wv.shape), _const_spec(cw.shape),
                  _const_spec(cb.shape), _const_spec(wd.shape)],
        out_specs=tile,
        out_shape=jax.ShapeDtypeStruct((batch, seq, D_MODEL), F32),
        scratch_shapes=[pltpu.VMEM((D_FF // fc, SUBLANES, fc), F32), pltpu.VMEM((tm, D_FF), BF16)],
        compiler_params=_params(("parallel", "arbitrary")),
        name="ffn",
    )(h3, x3, wa, wv, cw, cb, wd)


def kernel(x, mem, norm_mix_g, norm_mem_g, w_in, hgrn_lb_logits, hgrn_norm_g, fox_f_bias, fox_q_norm_g,
           fox_k_norm_g, mem_kv_w, mem_q_norm_g, mem_k_norm_g, w_br_hgrn, w_br_fox, w_br_mem, w_out,
           norm_ffn_g, ffn_w_up, ffn_conv_w, ffn_conv_b, ffn_w_down):
    batch, seq, _ = x.shape
    n_mem = mem.shape[1]
    depth = w_in.shape[0]
    assert depth == 1 and seq % 512 == 0

    x2 = x.reshape(batch * seq, D_MODEL)
    for l in range(depth):
        w = w_in[l]
        c_hg, c_fox = 4 * HG_WIDTH, 3 * FOX_WIDTH
        o_ff = c_hg + c_fox
        o_mq = o_ff + FOX_HEADS
        o_gate = o_mq + MEM_WIDTH
        whg = w[:, :c_hg].astype(BF16)
        wfox = w[:, c_hg:o_ff].astype(BF16)
        wff = jnp.pad(w[:, o_ff:o_mq], ((0, 0), (0, LANES - FOX_HEADS))).astype(BF16)
        wmq = w[:, o_mq:o_gate].astype(BF16)
        wgate = w[:, o_gate:].astype(BF16)
        row = lambda v: v.reshape(1, -1).astype(F32)

        zhg, zfox, zmq, zgate, zff = _in_proj(x2, row(norm_mix_g[l]), whg, wfox, wmq, wgate, wff, tm=512)

        ya = _hgrn(zhg, hgrn_lb_logits.astype(F32), row(hgrn_norm_g[l]), batch, seq)

        zfox3 = zfox.reshape(batch, seq, c_fox)
        fb = jnp.pad(row(fox_f_bias[l]), ((0, 0), (0, LANES - FOX_HEADS)))
        head_id = jnp.arange(FOX_WIDTH) // FOX_DH
        bd = (head_id[:, None] == head_id[None, :]).astype(BF16) * (1.0 / FOX_DH)
        qa, ka = _fox_prep(zfox3, zff.reshape(batch, seq, LANES), fb,
                           jnp.tile(row(fox_q_norm_g[l]), (1, FOX_HEADS)),
                           jnp.tile(row(fox_k_norm_g[l]), (1, FOX_HEADS)), bd, batch, seq)
        yb = _fox_attn(qa, ka, zfox3, batch, seq, tq=512)

        mk, mv = _mem_kv(mem.reshape(batch * n_mem, D_MODEL), row(norm_mem_g[l]), mem_kv_w[l].astype(BF16),
                         row(mem_k_norm_g[l]), tm=512)
        yc = _mem_attn(zmq.reshape(batch, seq, MEM_WIDTH), mk.reshape(batch, n_mem, MEM_WIDTH),
                       mv.reshape(batch, n_mem, MEM_WIDTH), row(mem_q_norm_g[l]), batch, seq, tq=512)

        x1, h2 = _merge(x2, ya.reshape(batch * seq, HG_WIDTH), yb.reshape(batch * seq, FOX_WIDTH),
                        yc.reshape(batch * seq, MEM_WIDTH), zgate, w_br_hgrn[l].astype(BF16),
                        w_br_fox[l].astype(BF16), w_br_mem[l].astype(BF16), w_out[l].astype(BF16),
                        row(norm_ffn_g[l]), tm=512)

        wup = ffn_w_up[l]
        out = _ffn(h2, x1, wup[:, :D_FF].astype(BF16), wup[:, D_FF:].astype(BF16), ffn_conv_w[l].astype(F32),
                   row(ffn_conv_b[l]), ffn_w_down[l].astype(BF16), batch, seq, tm=512, fc=256)
        x2 = out.reshape(batch * seq, D_MODEL)
    return x2.reshape(batch, seq, D_MODEL)
```

```python
import functools

import jax
import jax.numpy as jnp
from jax import lax
from jax.experimental import pallas as pl
from jax.experimental.pallas import tpu as pltpu

F32 = jnp.float32
BF16 = jnp.bfloat16

D_MODEL = 1024
EPS = 1e-6
CHUNK = 64
SUB = 16
HG_HEADS = 4
HG_D = 128
HG_WIDTH = HG_HEADS * HG_D
FOX_HEADS = 8
FOX_DH = 64
FOX_WIDTH = FOX_HEADS * FOX_DH
MEM_HEADS = 4
MEM_DH = 128
MEM_WIDTH = MEM_HEADS * MEM_DH
N_BRANCH = 3
D_FF = 2816
CONV_W = 3

LANES = 128
SUBLANES = 8
VMEM_LIMIT = 56 * 1024 * 1024

NEG_BIG = -1e30


def _const_spec(shape):
    nd = len(shape)
    return pl.BlockSpec(shape, lambda *_: (0,) * nd, pipeline_mode=pl.Buffered(1))


def _params(sem):
    return pltpu.CompilerParams(dimension_semantics=sem, vmem_limit_bytes=VMEM_LIMIT)


def _rms_rows(x, g):
    return x * lax.rsqrt(jnp.mean(x * x, axis=-1, keepdims=True) + EPS) * g


def _sigmoid(x):
    return 1.0 / (1.0 + jnp.exp(-x))


def _silu(x):
    return x * _sigmoid(x)


def _inproj_kernel(x_ref, g_ref, whg_ref, wfox_ref, wmq_ref, wgate_ref, wff_ref,
                   zhg_ref, zfox_ref, zmq_ref, zgate_ref, zff_ref, *, tn):
    h = _rms_rows(x_ref[...], g_ref[...]).astype(BF16)
    for w_ref, o_ref in ((whg_ref, zhg_ref), (wfox_ref, zfox_ref), (wmq_ref, zmq_ref),
                         (wgate_ref, zgate_ref), (wff_ref, zff_ref)):
        n = w_ref.shape[1]
        step = min(tn, n)
        for c in range(0, n, step):
            o_ref[:, c:c + step] = jnp.dot(h, w_ref[:, c:c + step],
                                           preferred_element_type=F32).astype(o_ref.dtype)


def _in_proj(x2, g, whg, wfox, wmq, wgate, wff, tm):
    n_rows = x2.shape[0]
    ws = (whg, wfox, wmq, wgate, wff)
    out_dtypes = (BF16, BF16, BF16, BF16, F32)
    row = lambda i: (i, 0)
    return pl.pallas_call(
        functools.partial(_inproj_kernel, tn=512),
        grid=(n_rows // tm,),
        in_specs=[pl.BlockSpec((tm, D_MODEL), row), _const_spec(g.shape)]
                 + [_const_spec(w.shape) for w in ws],
        out_specs=[pl.BlockSpec((tm, w.shape[1]), row) for w in ws],
        out_shape=[jax.ShapeDtypeStruct((n_rows, w.shape[1]), dt) for w, dt in zip(ws, out_dtypes)],
        compiler_params=_params(("parallel",)),
        name="in_proj",
    )(x2, g, *ws)


HG_LEVELS = (32, 16, 8, 4, 2, 1)
LOG2E = 1.4426950408889634


def _hgrn_pivot(g, h, row):
    n, w = g.shape
    if h >= SUBLANES:
        parts = [jnp.broadcast_to(g[p:p + 1], (2 * h, w)) for p in range(h - 1, n, 2 * h)]
        return parts[0] if len(parts) == 1 else jnp.concatenate(parts, axis=0)
    if h == 1:
        return jnp.where((row & 1) != 0, pltpu.roll(g, 1, axis=0), g)
    parts = []
    row8 = lax.broadcasted_iota(jnp.int32, (SUBLANES, w), 0)
    for lo in range(0, n, SUBLANES):
        cands = [jnp.broadcast_to(g[p:p + 1], (SUBLANES, w)) for p in range(lo + h - 1, lo + SUBLANES, 2 * h)]
        piv = cands[-1]
        for j in range(len(cands) - 2, -1, -1):
            piv = jnp.where(row8 < (j + 1) * 2 * h, cands[j], piv)
        parts.append(piv)
    return jnp.concatenate(parts, axis=0)


def _hgrn_kernel(q_ref, f_ref, i_ref, go_ref, lbl_ref, ng_ref, o_ref, st_ref, *, seq):
    lbl = lbl_ref[...]
    e = jnp.exp(lbl - jnp.max(lbl, axis=0, keepdims=True))
    lb = e[0:1] / jnp.sum(e, axis=0, keepdims=True)
    ng = ng_ref[...]
    st_ref[...] = jnp.zeros_like(st_ref)

    row = lax.broadcasted_iota(jnp.int32, (CHUNK, HG_WIDTH), 0)
    row_a = lax.broadcasted_iota(jnp.int32, (CHUNK, CHUNK), 0)
    col_a = lax.broadcasted_iota(jnp.int32, (CHUNK, CHUNK), 1)
    nt = (((1,), (1,)), ((), ()))
    heads = [slice(hd * HG_D, (hd + 1) * HG_D) for hd in range(HG_HEADS)]

    def chunk(c, carry):
        rows = pl.ds(pl.multiple_of(c * CHUNK, CHUNK), CHUNK)
        f = lb + (1.0 - lb) * _sigmoid(f_ref[rows, :].astype(F32))
        k_all = 1.0 - f
        g_all = jnp.log(f) * LOG2E
        sh = 1
        while sh < CHUNK:
            g_all = g_all + jnp.where(row >= sh, pltpu.roll(g_all, sh, axis=0), 0.0)
            sh *= 2
        q_all = _silu(q_ref[rows, :].astype(F32))
        go_all = _silu(go_ref[rows, :].astype(F32))
        i_all = i_ref[rows, :]

        a = [None] * HG_HEADS
        for h in HG_LEVELS:
            is_q = (row & h) != 0
            d = jnp.exp2(-jnp.abs(g_all - _hgrn_pivot(g_all, h, row)))
            q_t = jnp.where(is_q, q_all * d, 0.0).astype(BF16)
            k_t = jnp.where(is_q, 0.0, k_all * d).astype(BF16)
            same_blk = (row_a // (2 * h)) == (col_a // (2 * h))
            for hd, sl in enumerate(heads):
                prod = lax.dot_general(q_t[:, sl], k_t[:, sl], nt, preferred_element_type=F32)
                a[hd] = prod if a[hd] is None else jnp.where(same_blk, prod, a[hd])
        qk = q_all * k_all
        for hd, sl in enumerate(heads):
            a[hd] = jnp.where(row_a == col_a, jnp.sum(qk[:, sl], axis=-1, keepdims=True), a[hd]).astype(BF16)

        g_last = g_all[CHUNK - 1:CHUNK]
        q_in = (q_all * jnp.exp2(g_all)).astype(BF16)
        k_up = (k_all * jnp.exp2(g_last - g_all)).astype(BF16)
        dec = jnp.exp2(g_last)
        for hd, sl in enumerate(heads):
            st = st_ref[hd]
            o = lax.dot_general(q_in[:, sl], st.astype(BF16), nt, preferred_element_type=F32)
            o = o + jnp.dot(a[hd], i_all[:, sl], preferred_element_type=F32)
            i_t = i_all[:, sl].astype(F32).T.astype(BF16)
            st_ref[hd] = st * dec[:, sl] + jnp.dot(i_t, k_up[:, sl], preferred_element_type=F32)
            o_ref[rows, sl] = (_rms_rows(o, ng) * go_all[:, sl]).astype(o_ref.dtype)
        return carry

    lax.fori_loop(0, seq // CHUNK, chunk, 0)


def _hgrn(zhg, lb_logits, norm_g, batch, seq):
    zhg3 = zhg.reshape(batch, seq, 4 * HG_WIDTH)
    blk = lambda off: pl.BlockSpec((None, seq, HG_WIDTH), lambda b: (b, 0, off))
    return pl.pallas_call(
        functools.partial(_hgrn_kernel, seq=seq),
        grid=(batch,),
        in_specs=[blk(0), blk(1), blk(2), blk(3), _const_spec(lb_logits.shape), _const_spec(norm_g.shape)],
        out_specs=pl.BlockSpec((None, seq, HG_WIDTH), lambda b: (b, 0, 0)),
        out_shape=jax.ShapeDtypeStruct((batch, seq, HG_WIDTH), BF16),
        scratch_shapes=[pltpu.VMEM((HG_HEADS, HG_D, HG_D), F32)],
        compiler_params=_params(("parallel",)),
        name="hgrn",
    )(zhg3, zhg3, zhg3, zhg3, lb_logits, norm_g)


FOX_AUG = FOX_DH


def _split3(x):
    hi = x.astype(BF16).astype(F32)
    r = x - hi
    lo = r.astype(BF16).astype(F32)
    return hi, lo, r - lo


def _foxprep_kernel(q_ref, k_ref, ff_ref, fb_ref, qg_ref, kg_ref, bd_ref, qa_ref, ka_ref, *, seq):
    x = ff_ref[...] + fb_ref[...]
    fc = -(jnp.maximum(-x, 0.0) + jnp.log1p(jnp.exp(-jnp.abs(x))))
    pos = lax.broadcasted_iota(jnp.int32, (seq, LANES), 0)
    sh = 1
    while sh < seq:
        fc = fc + jnp.where(pos >= sh, pltpu.roll(fc, sh, axis=0), 0.0)
        sh *= 2
    fc = fc * LOG2E

    def normed(ref, g_ref, scale):
        v = ref[...].astype(F32)
        ms = jnp.dot((v * v).astype(BF16), bd_ref[...], preferred_element_type=F32)
        return v * lax.rsqrt(ms + EPS) * (g_ref[...] * scale)

    qn = normed(q_ref, qg_ref, FOX_DH ** -0.5 * LOG2E)
    kn = normed(k_ref, kg_ref, 1.0)
    lane = lax.broadcasted_iota(jnp.int32, (seq, LANES), 1)
    one = jnp.ones((seq, LANES), F32)
    zero = jnp.zeros((seq, LANES), F32)
    for h in range(FOX_HEADS):
        hi, lo, lo2 = _split3(jnp.broadcast_to(fc[:, h:h + 1], (seq, LANES)))
        a, b, c = FOX_AUG, FOX_AUG + 1, FOX_AUG + 2
        is_ones_q = (lane >= FOX_AUG + 3) & (lane < FOX_AUG + 6)
        is_ones_k = (lane >= FOX_AUG) & (lane < FOX_AUG + 3)
        aug_q = jnp.where(lane == a, hi, jnp.where(lane == b, lo, jnp.where(lane == c, lo2,
                          jnp.where(is_ones_q, one, zero))))
        aug_k = jnp.where(lane == a + 3, -hi, jnp.where(lane == b + 3, -lo, jnp.where(lane == c + 3, -lo2,
                          jnp.where(is_ones_k, one, zero))))
        pair = slice(LANES * (h // 2), LANES * (h // 2) + LANES)
        pq, pk = qn[:, pair], kn[:, pair]
        if h % 2:
            pq = pltpu.roll(pq, FOX_DH, axis=1)
            pk = pltpu.roll(pk, FOX_DH, axis=1)
        qa_ref[h] = jnp.where(lane < FOX_DH, pq, aug_q).astype(BF16)
        ka_ref[h] = jnp.where(lane < FOX_DH, pk, aug_k).astype(BF16)


def _fox_prep(zfox3, zff3, fb, qg, kg, bd, batch, seq):
    head_out = pl.BlockSpec((None, FOX_HEADS, seq, LANES), lambda b: (b, 0, 0, 0))
    return pl.pallas_call(
        functools.partial(_foxprep_kernel, seq=seq),
        grid=(batch,),
        in_specs=[pl.BlockSpec((None, seq, FOX_WIDTH), lambda b: (b, 0, 0)),
                  pl.BlockSpec((None, seq, FOX_WIDTH), lambda b: (b, 0, 1)),
                  pl.BlockSpec((None, seq, LANES), lambda b: (b, 0, 0)),
                  _const_spec(fb.shape), _const_spec(qg.shape), _const_spec(kg.shape), _const_spec(bd.shape)],
        out_specs=[head_out, head_out],
        out_shape=[jax.ShapeDtypeStruct((batch, FOX_HEADS, seq, LANES), BF16)] * 2,
        compiler_params=_params(("parallel",)),
        name="fox_prep",
    )(zfox3, zfox3, zff3, fb, qg, kg, bd)


def _foxattn_kernel(qa_ref, ka_ref, v_ref, o_ref, *, tq):
    i = pl.program_id(2)
    lane = lax.broadcasted_iota(jnp.int32, (tq, LANES), 1)
    rowi = lax.broadcasted_iota(jnp.int32, (tq, tq), 0)
    coli = lax.broadcasted_iota(jnp.int32, (tq, tq), 1)

    def step(j, carry, masked):
        kv_rows = pl.ds(pl.multiple_of(j * tq, tq), tq)
        vs = v_ref[kv_rows, :]
        new = []
        for hh in range(2):
            m, l, acc = carry[hh]
            s = lax.dot_general(qa_ref[hh], ka_ref[hh, kv_rows, :], (((1,), (1,)), ((), ())),
                                preferred_element_type=F32)
            if masked:
                s = jnp.where(rowi >= coli, s, NEG_BIG)
            m_new = jnp.maximum(m, jnp.max(s, axis=-1, keepdims=True))
            alpha = jnp.exp2(m - m_new)
            p = jnp.exp2(s - m_new)
            p_sum = p[:, :LANES]
            for k in range(1, tq // LANES):
                p_sum = p_sum + p[:, k * LANES:(k + 1) * LANES]
            acc = alpha * acc + jnp.dot(p.astype(BF16), vs, preferred_element_type=F32)
            new.append((m_new, alpha * l + p_sum, acc))
        return tuple(new)

    head_init = (jnp.full((tq, 1), NEG_BIG, F32), jnp.zeros((tq, LANES), F32), jnp.zeros((tq, LANES), F32))
    carry = lax.fori_loop(0, i, lambda j, c: step(j, c, False), (head_init, head_init))
    carry = step(i, carry, True)
    outs = [acc / jnp.sum(l, axis=-1, keepdims=True) for _, l, acc in carry]
    o_ref[...] = jnp.where(lane < FOX_DH, outs[0], outs[1]).astype(o_ref.dtype)


def _fox_attn(qa, ka, zfox3, batch, seq, tq):
    v_col0 = 2 * FOX_WIDTH // LANES
    return pl.pallas_call(
        functools.partial(_foxattn_kernel, tq=tq),
        grid=(batch, FOX_HEADS // 2, seq // tq),
        in_specs=[pl.BlockSpec((None, 2, tq, LANES), lambda b, p, i: (b, p, i, 0)),
                  pl.BlockSpec((None, 2, seq, LANES), lambda b, p, i: (b, p, 0, 0)),
                  pl.BlockSpec((None, seq, LANES), lambda b, p, i: (b, 0, v_col0 + p))],
        out_specs=pl.BlockSpec((None, tq, LANES), lambda b, p, i: (b, i, p)),
        out_shape=jax.ShapeDtypeStruct((batch, seq, FOX_WIDTH), BF16),
        compiler_params=_params(("parallel", "parallel", "arbitrary")),
        name="fox_attn",
    )(qa, ka, zfox3)


def _memkv_kernel(m_ref, g_ref, w_ref, kg_ref, k_ref, v_ref):
    h = _rms_rows(m_ref[...], g_ref[...]).astype(BF16)
    kv = jnp.dot(h, w_ref[...], preferred_element_type=F32)
    for hd in range(MEM_HEADS):
        sl = slice(hd * MEM_DH, (hd + 1) * MEM_DH)
        k_ref[:, sl] = _rms_rows(kv[:, sl], kg_ref[...]).astype(k_ref.dtype)
    v_ref[...] = kv[:, MEM_WIDTH:].astype(v_ref.dtype)


def _mem_kv(mem2, g, w, kg, tm):
    n_rows = mem2.shape[0]
    row = lambda i: (i, 0)
    return pl.pallas_call(
        _memkv_kernel,
        grid=(n_rows // tm,),
        in_specs=[pl.BlockSpec((tm, D_MODEL), row), _const_spec(g.shape), _const_spec(w.shape),
                  _const_spec(kg.shape)],
        out_specs=[pl.BlockSpec((tm, MEM_WIDTH), row)] * 2,
        out_shape=[jax.ShapeDtypeStruct((n_rows, MEM_WIDTH), BF16)] * 2,
        compiler_params=_params(("parallel",)),
        name="mem_kv",
    )(mem2, g, w, kg)


def _memattn_kernel(q_ref, k_ref, v_ref, qg_ref, o_ref):
    for hd in range(MEM_HEADS):
        sl = slice(hd * MEM_DH, (hd + 1) * MEM_DH)
        q = _rms_rows(q_ref[:, sl].astype(F32), qg_ref[...] * (MEM_DH ** -0.5)).astype(BF16)
        s = lax.dot_general(q, k_ref[:, sl], (((1,), (1,)), ((), ())), preferred_element_type=F32)
        p = jnp.exp(s - jnp.max(s, axis=-1, keepdims=True))
        l = jnp.sum(p, axis=-1, keepdims=True)
        o = jnp.dot(p.astype(BF16), v_ref[:, sl], preferred_element_type=F32)
        o_ref[:, sl] = (o / l).astype(o_ref.dtype)


def _mem_attn(zmq3, mk3, mv3, qg, batch, seq, tq):
    n_mem = mk3.shape[1]
    kv_spec = pl.BlockSpec((None, n_mem, MEM_WIDTH), lambda b, i: (b, 0, 0))
    return pl.pallas_call(
        _memattn_kernel,
        grid=(batch, seq // tq),
        in_specs=[pl.BlockSpec((None, tq, MEM_WIDTH), lambda b, i: (b, i, 0)), kv_spec, kv_spec,
                  _const_spec(qg.shape)],
        out_specs=pl.BlockSpec((None, tq, MEM_WIDTH), lambda b, i: (b, i, 0)),
        out_shape=jax.ShapeDtypeStruct((batch, seq, MEM_WIDTH), BF16),
        compiler_params=_params(("parallel", "parallel")),
        name="mem_attn",
    )(zmq3, mk3, mv3, qg)


def _merge_kernel(x_ref, ya_ref, yb_ref, yc_ref, gt_ref, wa_ref, wb_ref, wc_ref, wo_ref, ng_ref,
                  x1_ref, h2_ref):
    merged = None
    for br, (y_ref, w_ref) in enumerate(((ya_ref, wa_ref), (yb_ref, wb_ref), (yc_ref, wc_ref))):
        gate = _sigmoid(gt_ref[:, br * D_MODEL:(br + 1) * D_MODEL].astype(F32))
        t = gate * jnp.dot(y_ref[...], w_ref[...], preferred_element_type=F32)
        merged = t if merged is None else merged + t
    x1 = x_ref[...] + jnp.dot(merged.astype(BF16), wo_ref[...], preferred_element_type=F32)
    x1_ref[...] = x1
    h2_ref[...] = _rms_rows(x1, ng_ref[...]).astype(h2_ref.dtype)


def _merge(x2, ya, yb, yc, zgate, wa, wb, wc, wo, ng, tm):
    n_rows = x2.shape[0]
    row = lambda i: (i, 0)
    rows = lambda w: pl.BlockSpec((tm, w), row)
    return pl.pallas_call(
        _merge_kernel,
        grid=(n_rows // tm,),
        in_specs=[rows(D_MODEL), rows(HG_WIDTH), rows(FOX_WIDTH), rows(MEM_WIDTH), rows(N_BRANCH * D_MODEL),
                  _const_spec(wa.shape), _const_spec(wb.shape), _const_spec(wc.shape), _const_spec(wo.shape),
                  _const_spec(ng.shape)],
        out_specs=[rows(D_MODEL), rows(D_MODEL)],
        out_shape=[jax.ShapeDtypeStruct((n_rows, D_MODEL), F32), jax.ShapeDtypeStruct((n_rows, D_MODEL), BF16)],
        compiler_params=_params(("parallel",)),
        name="merge",
    )(x2, ya, yb, yc, zgate, wa, wb, wc, wo, ng)


def _ffn_kernel(h_ref, x1_ref, wa_ref, wv_ref, cw_ref, cb_ref, wd_ref, o_ref, carry_ref, y_ref, *, tm, fc):
    @pl.when(pl.program_id(1) == 0)
    def _():
        carry_ref[...] = jnp.zeros_like(carry_ref)

    h = h_ref[...]
    row = lax.broadcasted_iota(jnp.int32, (tm, fc), 0)
    for c in range(D_FF // fc):
        sl = slice(c * fc, (c + 1) * fc)
        a = jnp.dot(h, wa_ref[:, sl], preferred_element_type=F32)
        v = jnp.dot(h, wv_ref[:, sl], preferred_element_type=F32)
        prev = carry_ref[c]
        p1 = prev[SUBLANES - 1:SUBLANES]
        p2 = prev[SUBLANES - 2:SUBLANES - 1]
        a1 = jnp.where(row == 0, p1, pltpu.roll(a, 1, axis=0))
        a2 = jnp.where(row == 0, p2, jnp.where(row == 1, p1, pltpu.roll(a, 2, axis=0)))
        carry_ref[c] = a[tm - SUBLANES:tm]
        z = a2 * cw_ref[0:1, sl] + a1 * cw_ref[1:2, sl] + a * cw_ref[2:3, sl] + cb_ref[:, sl]
        gelu = 0.5 * z * (1.0 + lax.erf(z * (2.0 ** -0.5)))
        y_ref[:, sl] = (gelu * v).astype(y_ref.dtype)
    o_ref[...] = x1_ref[...] + jnp.dot(y_ref[...], wd_ref[...], preferred_element_type=F32)


def _ffn(h2, x1, wa, wv, cw, cb, wd, batch, seq, tm, fc):
    h3 = h2.reshape(batch, seq, D_MODEL)
    x3 = x1.reshape(batch, seq, D_MODEL)
    tile = pl.BlockSpec((None, tm, D_MODEL), lambda b, t: (b, t, 0))
    return pl.pallas_call(
        functools.partial(_ffn_kernel, tm=tm, fc=fc),
        grid=(batch, seq // tm),
        in_specs=[tile, tile, _const_spec(wa.shape), _const_spec(wv.shape), _const_spec(cw.shape),
                  _const_spec(cb.shape), _const_spec(wd.shape)],
        out_specs=tile,
        out_shape=jax.ShapeDtypeStruct((batch, seq, D_MODEL), F32),
        scratch_shapes=[pltpu.VMEM((D_FF // fc, SUBLANES, fc), F32), pltpu.VMEM((tm, D_FF), BF16)],
        compiler_params=_params(("parallel", "arbitrary")),
        name="ffn",
    )(h3, x3, wa, wv, cw, cb, wd)


def kernel(x, mem, norm_mix_g, norm_mem_g, w_in, hgrn_lb_logits, hgrn_norm_g, fox_f_bias, fox_q_norm_g,
           fox_k_norm_g, mem_kv_w, mem_q_norm_g, mem_k_norm_g, w_br_hgrn, w_br_fox, w_br_mem, w_out,
           norm_ffn_g, ffn_w_up, ffn_conv_w, ffn_conv_b, ffn_w_down):
    batch, seq, _ = x.shape
    n_mem = mem.shape[1]
    depth = w_in.shape[0]
    assert depth == 1 and seq % 512 == 0

    x2 = x.reshape(batch * seq, D_MODEL)
    for l in range(depth):
        w = w_in[l]
        c_hg, c_fox = 4 * HG_WIDTH, 3 * FOX_WIDTH
        o_ff = c_hg + c_fox
        o_mq = o_ff + FOX_HEADS
        o_gate = o_mq + MEM_WIDTH
        whg = w[:, :c_hg].astype(BF16)
        wfox = w[:, c_hg:o_ff].astype(BF16)
        wff = jnp.pad(w[:, o_ff:o_mq], ((0, 0), (0, LANES - FOX_HEADS))).astype(BF16)
        wmq = w[:, o_mq:o_gate].astype(BF16)
        wgate = w[:, o_gate:].astype(BF16)
        row = lambda v: v.reshape(1, -1).astype(F32)

        zhg, zfox, zmq, zgate, zff = _in_proj(x2, row(norm_mix_g[l]), whg, wfox, wmq, wgate, wff, tm=512)

        ya = _hgrn(zhg, hgrn_lb_logits.astype(F32), row(hgrn_norm_g[l]), batch, seq)

        zfox3 = zfox.reshape(batch, seq, c_fox)
        fb = jnp.pad(row(fox_f_bias[l]), ((0, 0), (0, LANES - FOX_HEADS)))
        head_id = jnp.arange(FOX_WIDTH) // FOX_DH
        bd = (head_id[:, None] == head_id[None, :]).astype(BF16) * (1.0 / FOX_DH)
        qa, ka = _fox_prep(zfox3, zff.reshape(batch, seq, LANES), fb,
                           jnp.tile(row(fox_q_norm_g[l]), (1, FOX_HEADS)),
                           jnp.tile(row(fox_k_norm_g[l]), (1, FOX_HEADS)), bd, batch, seq)
        yb = _fox_attn(qa, ka, zfox3, batch, seq, tq=512)

        mk, mv = _mem_kv(mem.reshape(batch * n_mem, D_MODEL), row(norm_mem_g[l]), mem_kv_w[l].astype(BF16),
                         row(mem_k_norm_g[l]), tm=512)
        yc = _mem_attn(zmq.reshape(batch, seq, MEM_WIDTH), mk.reshape(batch, n_mem, MEM_WIDTH),
                       mv.reshape(batch, n_mem, MEM_WIDTH), row(mem_q_norm_g[l]), batch, seq, tq=512)

        x1, h2 = _merge(x2, ya.reshape(batch * seq, HG_WIDTH), yb.reshape(batch * seq, FOX_WIDTH),
                        yc.reshape(batch * seq, MEM_WIDTH), zgate, w_br_hgrn[l].astype(BF16),
                        w_br_fox[l].astype(BF16), w_br_mem[l].astype(BF16), w_out[l].astype(BF16),
                        row(norm_ffn_g[l]), tm=512)

        wup = ffn_w_up[l]
        out = _ffn(h2, x1, wup[:, :D_FF].astype(BF16), wup[:, D_FF:].astype(BF16), ffn_conv_w[l].astype(F32),
                   row(ffn_conv_b[l]), ffn_w_down[l].astype(BF16), batch, seq, tm=512, fc=256)
        x2 = out.reshape(batch * seq, D_MODEL)
    return x2.reshape(batch, seq, D_MODEL)
```

```python
import functools

import numpy as np
import jax
import jax.numpy as jnp
from jax import lax
from jax.experimental import pallas as pl
from jax.experimental.pallas import tpu as pltpu

F32 = jnp.float32
BF16 = jnp.bfloat16

D_MODEL = 1024
EPS = 1e-6
CHUNK = 64
SUB = 16
HG_HEADS = 4
HG_D = 128
HG_WIDTH = HG_HEADS * HG_D
FOX_HEADS = 8
FOX_DH = 64
FOX_WIDTH = FOX_HEADS * FOX_DH
MEM_HEADS = 4
MEM_DH = 128
MEM_WIDTH = MEM_HEADS * MEM_DH
N_BRANCH = 3
D_FF = 2816
CONV_W = 3

LANES = 128
SUBLANES = 8
MXU_WIDTH = 256
VMEM_LIMIT = 56 * 1024 * 1024

NEG_BIG = -1e30


def _const_spec(shape):
    nd = len(shape)
    return pl.BlockSpec(shape, lambda *_: (0,) * nd, pipeline_mode=pl.Buffered(1))


def _params(sem):
    return pltpu.CompilerParams(dimension_semantics=sem, vmem_limit_bytes=VMEM_LIMIT)


def _rms_rows(x, g):
    return x * lax.rsqrt(jnp.mean(x * x, axis=-1, keepdims=True) + EPS) * g


def _sigmoid(x):
    return 1.0 / (1.0 + jnp.exp(-x))


def _silu(x):
    return x * _sigmoid(x)


def _inproj_kernel(x_ref, g_ref, whg_ref, wfox_ref, wmq_ref, wgate_ref, wff_ref,
                   zhg_ref, zfox_ref, zmq_ref, zgate_ref, zff_ref, *, tn):
    h = _rms_rows(x_ref[...], g_ref[...]).astype(BF16)
    for w_ref, o_ref in ((whg_ref, zhg_ref), (wfox_ref, zfox_ref), (wmq_ref, zmq_ref),
                         (wgate_ref, zgate_ref), (wff_ref, zff_ref)):
        n = w_ref.shape[1]
        step = min(tn, n)
        for c in range(0, n, step):
            o_ref[:, c:c + step] = jnp.dot(h, w_ref[:, c:c + step],
                                           preferred_element_type=F32).astype(o_ref.dtype)


def _in_proj(x2, g, whg, wfox, wmq, wgate, wff, tm):
    n_rows = x2.shape[0]
    ws = (whg, wfox, wmq, wgate, wff)
    out_dtypes = (BF16, BF16, BF16, BF16, F32)
    row = lambda i: (i, 0)
    return pl.pallas_call(
        functools.partial(_inproj_kernel, tn=512),
        grid=(n_rows // tm,),
        in_specs=[pl.BlockSpec((tm, D_MODEL), row), _const_spec(g.shape)]
                 + [_const_spec(w.shape) for w in ws],
        out_specs=[pl.BlockSpec((tm, w.shape[1]), row) for w in ws],
        out_shape=[jax.ShapeDtypeStruct((n_rows, w.shape[1]), dt) for w, dt in zip(ws, out_dtypes)],
        compiler_params=_params(("parallel",)),
        name="in_proj",
    )(x2, g, *ws)


HG_LEVELS = (32, 16, 8, 4, 2, 1)
LOG2E = 1.4426950408889634


def _hgrn_pivot(g, h, row):
    n, w = g.shape
    if h >= SUBLANES:
        parts = [jnp.broadcast_to(g[p:p + 1], (2 * h, w)) for p in range(h - 1, n, 2 * h)]
        return parts[0] if len(parts) == 1 else jnp.concatenate(parts, axis=0)
    if h == 1:
        return jnp.where((row & 1) != 0, pltpu.roll(g, 1, axis=0), g)
    parts = []
    row8 = lax.broadcasted_iota(jnp.int32, (SUBLANES, w), 0)
    for lo in range(0, n, SUBLANES):
        cands = [jnp.broadcast_to(g[p:p + 1], (SUBLANES, w)) for p in range(lo + h - 1, lo + SUBLANES, 2 * h)]
        piv = cands[-1]
        for j in range(len(cands) - 2, -1, -1):
            piv = jnp.where(row8 < (j + 1) * 2 * h, cands[j], piv)
        parts.append(piv)
    return jnp.concatenate(parts, axis=0)


def _hgrn_kernel(q_ref, f_ref, i_ref, go_ref, lbl_ref, ng_ref, o_ref, st_ref, *, seq):
    lbl = lbl_ref[...]
    e = jnp.exp(lbl - jnp.max(lbl, axis=0, keepdims=True))
    lb = e[0:1] / jnp.sum(e, axis=0, keepdims=True)
    ng = ng_ref[...]
    st_ref[...] = jnp.zeros_like(st_ref)

    row = lax.broadcasted_iota(jnp.int32, (CHUNK, HG_WIDTH), 0)
    row_a = lax.broadcasted_iota(jnp.int32, (CHUNK, CHUNK), 0)
    col_a = lax.broadcasted_iota(jnp.int32, (CHUNK, CHUNK), 1)
    nt = (((1,), (1,)), ((), ()))
    heads = [slice(hd * HG_D, (hd + 1) * HG_D) for hd in range(HG_HEADS)]

    def chunk(c, carry):
        rows = pl.ds(pl.multiple_of(c * CHUNK, CHUNK), CHUNK)
        f = lb + (1.0 - lb) * _sigmoid(f_ref[rows, :].astype(F32))
        k_all = 1.0 - f
        g_all = jnp.log(f) * LOG2E
        sh = 1
        while sh < CHUNK:
            g_all = g_all + jnp.where(row >= sh, pltpu.roll(g_all, sh, axis=0), 0.0)
            sh *= 2
        q_all = _silu(q_ref[rows, :].astype(F32))
        go_all = _silu(go_ref[rows, :].astype(F32))
        i_all = i_ref[rows, :]

        a = [None] * HG_HEADS
        for h in HG_LEVELS:
            is_q = (row & h) != 0
            d = jnp.exp2(-jnp.abs(g_all - _hgrn_pivot(g_all, h, row)))
            q_t = jnp.where(is_q, q_all * d, 0.0).astype(BF16)
            k_t = jnp.where(is_q, 0.0, k_all * d).astype(BF16)
            same_blk = (row_a // (2 * h)) == (col_a // (2 * h))
            for hd, sl in enumerate(heads):
                prod = lax.dot_general(q_t[:, sl], k_t[:, sl], nt, preferred_element_type=F32)
                a[hd] = prod if a[hd] is None else jnp.where(same_blk, prod, a[hd])
        qk = q_all * k_all
        for hd, sl in enumerate(heads):
            a[hd] = jnp.where(row_a == col_a, jnp.sum(qk[:, sl], axis=-1, keepdims=True), a[hd]).astype(BF16)

        g_last = g_all[CHUNK - 1:CHUNK]
        q_in = (q_all * jnp.exp2(g_all)).astype(BF16)
        k_up = (k_all * jnp.exp2(g_last - g_all)).astype(BF16)
        dec = jnp.exp2(g_last)
        for hd, sl in enumerate(heads):
            st = st_ref[hd]
            o = lax.dot_general(q_in[:, sl], st.astype(BF16), nt, preferred_element_type=F32)
            o = o + jnp.dot(a[hd], i_all[:, sl], preferred_element_type=F32)
            i_t = i_all[:, sl].astype(F32).T.astype(BF16)
            st_ref[hd] = st * dec[:, sl] + jnp.dot(i_t, k_up[:, sl], preferred_element_type=F32)
            o_ref[rows, sl] = (_rms_rows(o, ng) * go_all[:, sl]).astype(o_ref.dtype)
        return carry

    lax.fori_loop(0, seq // CHUNK, chunk, 0)


def _hgrn(zhg, lb_logits, norm_g, batch, seq):
    zhg3 = zhg.reshape(batch, seq, 4 * HG_WIDTH)
    blk = lambda off: pl.BlockSpec((None, seq, HG_WIDTH), lambda b: (b, 0, off))
    return pl.pallas_call(
        functools.partial(_hgrn_kernel, seq=seq),
        grid=(batch,),
        in_specs=[blk(0), blk(1), blk(2), blk(3), _const_spec(lb_logits.shape), _const_spec(norm_g.shape)],
        out_specs=pl.BlockSpec((None, seq, HG_WIDTH), lambda b: (b, 0, 0)),
        out_shape=jax.ShapeDtypeStruct((batch, seq, HG_WIDTH), BF16),
        scratch_shapes=[pltpu.VMEM((HG_HEADS, HG_D, HG_D), F32)],
        compiler_params=_params(("parallel",)),
        name="hgrn",
    )(zhg3, zhg3, zhg3, zhg3, lb_logits, norm_g)


FOX_AUG = FOX_DH
FOX_NAUG = 6


def _split3(x):
    hi = x.astype(BF16).astype(F32)
    r = x - hi
    lo = r.astype(BF16).astype(F32)
    return hi, lo, r - lo


def _foxprep_kernel(q_ref, k_ref, v_ref, ff_ref, fb_ref, qg_ref, kg_ref, bd_ref, place_ref,
                    qa_ref, ka_ref, vt_ref, *, seq):
    for p in range(FOX_HEADS // 2):
        vt_ref[p] = v_ref[:, p * LANES:(p + 1) * LANES].astype(F32).T.astype(BF16)
    x = ff_ref[...] + fb_ref[...]
    fc = -(jnp.maximum(-x, 0.0) + jnp.log1p(jnp.exp(-jnp.abs(x))))
    pos = lax.broadcasted_iota(jnp.int32, (seq, LANES), 0)
    sh = 1
    while sh < seq:
        fc = fc + jnp.where(pos >= sh, pltpu.roll(fc, sh, axis=0), 0.0)
        sh *= 2
    fc = fc * LOG2E

    def normed(ref, g_ref, scale):
        v = ref[...].astype(F32)
        sq = (v * v).astype(BF16)
        w = bd_ref.shape[0]
        ms = jnp.concatenate([jnp.dot(sq[:, c:c + w], bd_ref[...], preferred_element_type=F32)
                              for c in range(0, FOX_WIDTH, w)], axis=1)
        return v * lax.rsqrt(ms + EPS) * (g_ref[...] * scale)

    qn = normed(q_ref, qg_ref, FOX_DH ** -0.5 * LOG2E)
    kn = normed(k_ref, kg_ref, 1.0)
    lane = lax.broadcasted_iota(jnp.int32, (seq, LANES), 1)
    hi, lo, lo2 = _split3(fc)
    hi = jnp.where(lane == FOX_HEADS, 1.0, hi)
    aug = jnp.dot(jnp.concatenate([hi, lo, lo2], axis=1).astype(BF16), place_ref[...],
                  preferred_element_type=F32)
    for h in range(FOX_HEADS):
        aug_q = pltpu.roll(aug, (FOX_AUG - FOX_NAUG * h) % LANES, axis=1)
        aug_k = pltpu.roll(aug, (FOX_AUG - FOX_DH - FOX_NAUG * h) % LANES, axis=1) if h else aug
        pair = slice(LANES * (h // 2), LANES * (h // 2) + LANES)
        pq, pk = qn[:, pair], kn[:, pair]
        if h % 2:
            pq = pltpu.roll(pq, FOX_DH, axis=1)
            pk = pltpu.roll(pk, FOX_DH, axis=1)
        qa_ref[h] = jnp.where(lane < FOX_DH, pq, aug_q).astype(BF16)
        ka_ref[h] = jnp.where(lane < FOX_DH, pk, jnp.where(lane < FOX_AUG + FOX_NAUG, aug_k, 0.0)).astype(BF16)


def _fox_place_matrix():
    place = np.zeros((3 * LANES, LANES), np.float32)
    for h in range(FOX_HEADS):
        for part in range(3):
            place[part * LANES + h, FOX_NAUG * h + part] = 1.0
            place[part * LANES + h, FOX_DH + FOX_NAUG * h + 3 + part] = -1.0
            place[FOX_HEADS, FOX_NAUG * h + 3 + part] = 1.0
            place[FOX_HEADS, FOX_DH + FOX_NAUG * h + part] = 1.0
    return jnp.asarray(place, BF16)


def _fox_prep(zfox3, zff3, fb, qg, kg, bd, batch, seq):
    place = _fox_place_matrix()
    head_out = pl.BlockSpec((None, FOX_HEADS, seq, LANES), lambda b: (b, 0, 0, 0))
    return pl.pallas_call(
        functools.partial(_foxprep_kernel, seq=seq),
        grid=(batch,),
        in_specs=[pl.BlockSpec((None, seq, FOX_WIDTH), lambda b: (b, 0, 0)),
                  pl.BlockSpec((None, seq, FOX_WIDTH), lambda b: (b, 0, 1)),
                  pl.BlockSpec((None, seq, FOX_WIDTH), lambda b: (b, 0, 2)),
                  pl.BlockSpec((None, seq, LANES), lambda b: (b, 0, 0)),
                  _const_spec(fb.shape), _const_spec(qg.shape), _const_spec(kg.shape), _const_spec(bd.shape),
                  _const_spec(place.shape)],
        out_specs=[head_out, head_out,
                   pl.BlockSpec((None, FOX_HEADS // 2, LANES, seq), lambda b: (b, 0, 0, 0))],
        out_shape=[jax.ShapeDtypeStruct((batch, FOX_HEADS, seq, LANES), BF16)] * 2
                  + [jax.ShapeDtypeStruct((batch, FOX_HEADS // 2, LANES, seq), BF16)],
        compiler_params=_params(("parallel",)),
        name="fox_prep",
    )(zfox3, zfox3, zfox3, zff3, fb, qg, kg, bd, place)


def _foxattn_kernel(qa_ref, ka_ref, vt_ref, o_ref, s_ref, m_ref, l_ref, acc_ref, *, tq, n_blk):
    key = lax.broadcasted_iota(jnp.int32, (tq, tq), 0)
    qry = lax.broadcasted_iota(jnp.int32, (tq, tq), 1)
    nt = (((1,), (1,)), ((), ()))
    steps = [(i, j) for i in range(n_blk) for j in range(i + 1)]

    def scores(n, hh):
        i, j = steps[n]
        s_ref[n % 2, hh] = lax.dot_general(ka_ref[hh, j * tq:(j + 1) * tq, :], qa_ref[hh, i * tq:(i + 1) * tq, :],
                                           nt, preferred_element_type=F32)

    def reduce(n, hh):
        i, j = steps[n]
        s = s_ref[n % 2, hh]
        if j == i:
            s = jnp.where(key <= qry, s, NEG_BIG)
        m_new = jnp.max(s, axis=0, keepdims=True)
        vt = vt_ref[hh * FOX_DH:(hh + 1) * FOX_DH, j * tq:(j + 1) * tq]
        if j == 0:
            p = jnp.exp2(s - m_new)
            m_ref[hh] = m_new
            l_ref[hh] = jnp.sum(p, axis=0, keepdims=True)
            acc_ref[hh] = jnp.dot(vt, p.astype(BF16), preferred_element_type=F32)
        else:
            m = m_ref[hh]
            m_new = jnp.maximum(m, m_new)
            alpha = jnp.exp2(m - m_new)
            p = jnp.exp2(s - m_new)
            m_ref[hh] = m_new
            l_ref[hh] = alpha * l_ref[hh] + jnp.sum(p, axis=0, keepdims=True)
            acc_ref[hh] = alpha * acc_ref[hh] + jnp.dot(vt, p.astype(BF16), preferred_element_type=F32)

    for hh in range(2):
        scores(0, hh)
    for n, (i, j) in enumerate(steps):
        for hh in range(2):
            if n + 1 < len(steps):
                scores(n + 1, hh)
            reduce(n, hh)
        if j == i:
            out_t = jnp.concatenate([acc_ref[hh] / l_ref[hh] for hh in range(2)], axis=0)
            o_ref[i * tq:(i + 1) * tq, :] = out_t.T.astype(o_ref.dtype)


def _fox_attn(qa, ka, vt, batch, seq, tq):
    pair = lambda b, p: (b, p, 0, 0)
    return pl.pallas_call(
        functools.partial(_foxattn_kernel, tq=tq, n_blk=seq // tq),
        grid=(batch, FOX_HEADS // 2),
        in_specs=[pl.BlockSpec((None, 2, seq, LANES), pair), pl.BlockSpec((None, 2, seq, LANES), pair),
                  pl.BlockSpec((None, None, LANES, seq), pair)],
        out_specs=pl.BlockSpec((None, seq, LANES), lambda b, p: (b, 0, p)),
        out_shape=jax.ShapeDtypeStruct((batch, seq, FOX_WIDTH), BF16),
        scratch_shapes=[pltpu.VMEM((2, 2, tq, tq), F32), pltpu.VMEM((2, 1, tq), F32), pltpu.VMEM((2, 1, tq), F32),
                        pltpu.VMEM((2, FOX_DH, tq), F32)],
        compiler_params=_params(("parallel", "parallel")),
        name="fox_attn",
    )(qa, ka, vt)


def _memkv_kernel(m_ref, g_ref, w_ref, kg_ref, k_ref, v_ref):
    h = _rms_rows(m_ref[...], g_ref[...]).astype(BF16)
    kv = jnp.dot(h, w_ref[...], preferred_element_type=F32)
    for hd in range(MEM_HEADS):
        sl = slice(hd * MEM_DH, (hd + 1) * MEM_DH)
        k_ref[:, sl] = _rms_rows(kv[:, sl], kg_ref[...]).astype(k_ref.dtype)
    v_ref[...] = kv[:, MEM_WIDTH:].astype(v_ref.dtype)


def _mem_kv(mem2, g, w, kg, tm):
    n_rows = mem2.shape[0]
    row = lambda i: (i, 0)
    return pl.pallas_call(
        _memkv_kernel,
        grid=(n_rows // tm,),
        in_specs=[pl.BlockSpec((tm, D_MODEL), row), _const_spec(g.shape), _const_spec(w.shape),
                  _const_spec(kg.shape)],
        out_specs=[pl.BlockSpec((tm, MEM_WIDTH), row)] * 2,
        out_shape=[jax.ShapeDtypeStruct((n_rows, MEM_WIDTH), BF16)] * 2,
        compiler_params=_params(("parallel",)),
        name="mem_kv",
    )(mem2, g, w, kg)


def _memattn_kernel(q_ref, k_ref, v_ref, qg_ref, o_ref):
    for hd in range(MEM_HEADS):
        sl = slice(hd * MEM_DH, (hd + 1) * MEM_DH)
        q = _rms_rows(q_ref[:, sl].astype(F32), qg_ref[...] * (MEM_DH ** -0.5)).astype(BF16)
        s = lax.dot_general(q, k_ref[:, sl], (((1,), (1,)), ((), ())), preferred_element_type=F32)
        p = jnp.exp(s - jnp.max(s, axis=-1, keepdims=True))
        l = jnp.sum(p, axis=-1, keepdims=True)
        o = jnp.dot(p.astype(BF16), v_ref[:, sl], preferred_element_type=F32)
        o_ref[:, sl] = (o / l).astype(o_ref.dtype)


def _mem_attn(zmq3, mk3, mv3, qg, batch, seq, tq):
    n_mem = mk3.shape[1]
    kv_spec = pl.BlockSpec((None, n_mem, MEM_WIDTH), lambda b, i: (b, 0, 0))
    return pl.pallas_call(
        _memattn_kernel,
        grid=(batch, seq // tq),
        in_specs=[pl.BlockSpec((None, tq, MEM_WIDTH), lambda b, i: (b, i, 0)), kv_spec, kv_spec,
                  _const_spec(qg.shape)],
        out_specs=pl.BlockSpec((None, tq, MEM_WIDTH), lambda b, i: (b, i, 0)),
        out_shape=jax.ShapeDtypeStruct((batch, seq, MEM_WIDTH), BF16),
        compiler_params=_params(("parallel", "parallel")),
        name="mem_attn",
    )(zmq3, mk3, mv3, qg)


def _merge_kernel(x_ref, ya_ref, yb_ref, yc_ref, gt_ref, wa_ref, wb_ref, wc_ref, wo_ref, ng_ref,
                  x1_ref, h2_ref):
    merged = None
    for br, (y_ref, w_ref) in enumerate(((ya_ref, wa_ref), (yb_ref, wb_ref), (yc_ref, wc_ref))):
        gate = _sigmoid(gt_ref[:, br * D_MODEL:(br + 1) * D_MODEL].astype(F32))
        t = gate * jnp.dot(y_ref[...], w_ref[...], preferred_element_type=F32)
        merged = t if merged is None else merged + t
    x1 = x_ref[...] + jnp.dot(merged.astype(BF16), wo_ref[...], preferred_element_type=F32)
    x1_ref[...] = x1
    h2_ref[...] = _rms_rows(x1, ng_ref[...]).astype(h2_ref.dtype)


def _merge(x2, ya, yb, yc, zgate, wa, wb, wc, wo, ng, tm):
    n_rows = x2.shape[0]
    row = lambda i: (i, 0)
    rows = lambda w: pl.BlockSpec((tm, w), row)
    return pl.pallas_call(
        _merge_kernel,
        grid=(n_rows // tm,),
        in_specs=[rows(D_MODEL), rows(HG_WIDTH), rows(FOX_WIDTH), rows(MEM_WIDTH), rows(N_BRANCH * D_MODEL),
                  _const_spec(wa.shape), _const_spec(wb.shape), _const_spec(wc.shape), _const_spec(wo.shape),
                  _const_spec(ng.shape)],
        out_specs=[rows(D_MODEL), rows(D_MODEL)],
        out_shape=[jax.ShapeDtypeStruct((n_rows, D_MODEL), F32), jax.ShapeDtypeStruct((n_rows, D_MODEL), BF16)],
        compiler_params=_params(("parallel",)),
        name="merge",
    )(x2, ya, yb, yc, zgate, wa, wb, wc, wo, ng)


def _ffn_kernel(h_ref, x1_ref, wa_ref, wv_ref, cw_ref, cb_ref, wd_ref, o_ref, carry_ref, y_ref, *, tm, fc):
    @pl.when(pl.program_id(1) == 0)
    def _():
        carry_ref[...] = jnp.zeros_like(carry_ref)

    h = h_ref[...]
    row = lax.broadcasted_iota(jnp.int32, (tm, fc), 0)
    for c in range(D_FF // fc):
        sl = slice(c * fc, (c + 1) * fc)
        a = jnp.dot(h, wa_ref[:, sl], preferred_element_type=F32)
        v = jnp.dot(h, wv_ref[:, sl], preferred_element_type=F32)
        prev = carry_ref[c]
        p1 = prev[SUBLANES - 1:SUBLANES]
        p2 = prev[SUBLANES - 2:SUBLANES - 1]
        a1 = jnp.where(row == 0, p1, pltpu.roll(a, 1, axis=0))
        a2 = jnp.where(row == 0, p2, jnp.where(row == 1, p1, pltpu.roll(a, 2, axis=0)))
        carry_ref[c] = a[tm - SUBLANES:tm]
        z = a2 * cw_ref[0:1, sl] + a1 * cw_ref[1:2, sl] + a * cw_ref[2:3, sl] + cb_ref[:, sl]
        gelu = 0.5 * z * (1.0 + lax.erf(z * (2.0 ** -0.5)))
        y_ref[:, sl] = (gelu * v).astype(y_ref.dtype)
    o_ref[...] = x1_ref[...] + jnp.dot(y_ref[...], wd_ref[...], preferred_element_type=F32)


def _ffn(h2, x1, wa, wv, cw, cb, wd, batch, seq, tm, fc):
    h3 = h2.reshape(batch, seq, D_MODEL)
    x3 = x1.reshape(batch, seq, D_MODEL)
    tile = pl.BlockSpec((None, tm, D_MODEL), lambda b, t: (b, t, 0))
    return pl.pallas_call(
        functools.partial(_ffn_kernel, tm=tm, fc=fc),
        grid=(batch, seq // tm),
        in_specs=[tile, tile, _const_spec(wa.shape), _const_spec(wv.shape), _const_spec(cw.shape),
                  _const_spec(cb.shape), _const_spec(wd.shape)],
        out_specs=tile,
        out_shape=jax.ShapeDtypeStruct((batch, seq, D_MODEL), F32),
        scratch_shapes=[pltpu.VMEM((D_FF // fc, SUBLANES, fc), F32), pltpu.VMEM((tm, D_FF), BF16)],
        compiler_params=_params(("parallel", "arbitrary")),
        name="ffn",
    )(h3, x3, wa, wv, cw, cb, wd)


def kernel(x, mem, norm_mix_g, norm_mem_g, w_in, hgrn_lb_logits, hgrn_norm_g, fox_f_bias, fox_q_norm_g,
           fox_k_norm_g, mem_kv_w, mem_q_norm_g, mem_k_norm_g, w_br_hgrn, w_br_fox, w_br_mem, w_out,
           norm_ffn_g, ffn_w_up, ffn_conv_w, ffn_conv_b, ffn_w_down):
    batch, seq, _ = x.shape
    n_mem = mem.shape[1]
    depth = w_in.shape[0]
    assert depth == 1 and seq % 512 == 0

    x2 = x.reshape(batch * seq, D_MODEL)
    for l in range(depth):
        w = w_in[l]
        c_hg, c_fox = 4 * HG_WIDTH, 3 * FOX_WIDTH
        o_ff = c_hg + c_fox
        o_mq = o_ff + FOX_HEADS
        o_gate = o_mq + MEM_WIDTH
        whg = w[:, :c_hg].astype(BF16)
        wfox = w[:, c_hg:o_ff].astype(BF16)
        wff = jnp.pad(w[:, o_ff:o_mq], ((0, 0), (0, LANES - FOX_HEADS))).astype(BF16)
        wmq = w[:, o_mq:o_gate].astype(BF16)
        wgate = w[:, o_gate:].astype(BF16)
        row = lambda v: v.reshape(1, -1).astype(F32)

        zhg, zfox, zmq, zgate, zff = _in_proj(x2, row(norm_mix_g[l]), whg, wfox, wmq, wgate, wff, tm=512)

        ya = _hgrn(zhg, hgrn_lb_logits.astype(F32), row(hgrn_norm_g[l]), batch, seq)

        zfox3 = zfox.reshape(batch, seq, c_fox)
        fb = jnp.pad(row(fox_f_bias[l]), ((0, 0), (0, LANES - FOX_HEADS)))
        head_id = jnp.arange(MXU_WIDTH) // FOX_DH
        bd = (head_id[:, None] == head_id[None, :]).astype(BF16) * (1.0 / FOX_DH)
        qa, ka, vt = _fox_prep(zfox3, zff.reshape(batch, seq, LANES), fb,
                               jnp.tile(row(fox_q_norm_g[l]), (1, FOX_HEADS)),
                               jnp.tile(row(fox_k_norm_g[l]), (1, FOX_HEADS)), bd, batch, seq)
        yb = _fox_attn(qa, ka, vt, batch, seq, tq=512)

        mk, mv = _mem_kv(mem.reshape(batch * n_mem, D_MODEL), row(norm_mem_g[l]), mem_kv_w[l].astype(BF16),
                         row(mem_k_norm_g[l]), tm=512)
        yc = _mem_attn(zmq.reshape(batch, seq, MEM_WIDTH), mk.reshape(batch, n_mem, MEM_WIDTH),
                       mv.reshape(batch, n_mem, MEM_WIDTH), row(mem_q_norm_g[l]), batch, seq, tq=512)

        x1, h2 = _merge(x2, ya.reshape(batch * seq, HG_WIDTH), yb.reshape(batch * seq, FOX_WIDTH),
                        yc.reshape(batch * seq, MEM_WIDTH), zgate, w_br_hgrn[l].astype(BF16),
                        w_br_fox[l].astype(BF16), w_br_mem[l].astype(BF16), w_out[l].astype(BF16),
                        row(norm_ffn_g[l]), tm=512)

        wup = ffn_w_up[l]
        out = _ffn(h2, x1, wup[:, :D_FF].astype(BF16), wup[:, D_FF:].astype(BF16), ffn_conv_w[l].astype(F32),
                   row(ffn_conv_b[l]), ffn_w_down[l].astype(BF16), batch, seq, tm=512, fc=256)
        x2 = out.reshape(batch * seq, D_MODEL)
    return x2.reshape(batch, seq, D_MODEL)
```

```python
import functools

import numpy as np
import jax
import jax.numpy as jnp
from jax import lax
from jax.experimental import pallas as pl
from jax.experimental.pallas import tpu as pltpu

F32 = jnp.float32
BF16 = jnp.bfloat16

D_MODEL = 1024
EPS = 1e-6
CHUNK = 64
SUB = 16
HG_HEADS = 4
HG_D = 128
HG_WIDTH = HG_HEADS * HG_D
FOX_HEADS = 8
FOX_DH = 64
FOX_WIDTH = FOX_HEADS * FOX_DH
MEM_HEADS = 4
MEM_DH = 128
MEM_WIDTH = MEM_HEADS * MEM_DH
N_BRANCH = 3
D_FF = 2816
CONV_W = 3

LANES = 128
SUBLANES = 8
MXU_WIDTH = 256
VMEM_LIMIT = 56 * 1024 * 1024

NEG_BIG = -1e30


def _const_spec(shape):
    nd = len(shape)
    return pl.BlockSpec(shape, lambda *_: (0,) * nd, pipeline_mode=pl.Buffered(1))


def _params(sem):
    return pltpu.CompilerParams(dimension_semantics=sem, vmem_limit_bytes=VMEM_LIMIT)


def _rms_rows(x, g):
    return x * lax.rsqrt(jnp.mean(x * x, axis=-1, keepdims=True) + EPS) * g


def _sigmoid(x):
    return 1.0 / (1.0 + jnp.exp(-x))


def _silu(x):
    return x * _sigmoid(x)


def _inproj_kernel(x_ref, g_ref, whg_ref, wfox_ref, wmq_ref, wgate_ref, wff_ref,
                   zhg_ref, zfox_ref, zmq_ref, zgate_ref, zff_ref, *, tn):
    h = _rms_rows(x_ref[...], g_ref[...]).astype(BF16)
    for w_ref, o_ref in ((whg_ref, zhg_ref), (wfox_ref, zfox_ref), (wmq_ref, zmq_ref),
                         (wgate_ref, zgate_ref), (wff_ref, zff_ref)):
        n = w_ref.shape[1]
        step = min(tn, n)
        for c in range(0, n, step):
            o_ref[:, c:c + step] = jnp.dot(h, w_ref[:, c:c + step],
                                           preferred_element_type=F32).astype(o_ref.dtype)


def _in_proj(x2, g, whg, wfox, wmq, wgate, wff, tm):
    n_rows = x2.shape[0]
    ws = (whg, wfox, wmq, wgate, wff)
    out_dtypes = (BF16, BF16, BF16, BF16, F32)
    row = lambda i: (i, 0)
    return pl.pallas_call(
        functools.partial(_inproj_kernel, tn=512),
        grid=(n_rows // tm,),
        in_specs=[pl.BlockSpec((tm, D_MODEL), row), _const_spec(g.shape)]
                 + [_const_spec(w.shape) for w in ws],
        out_specs=[pl.BlockSpec((tm, w.shape[1]), row) for w in ws],
        out_shape=[jax.ShapeDtypeStruct((n_rows, w.shape[1]), dt) for w, dt in zip(ws, out_dtypes)],
        compiler_params=_params(("parallel",)),
        name="in_proj",
    )(x2, g, *ws)


HG_LEVELS = (32, 16, 8, 4, 2, 1)
LOG2E = 1.4426950408889634


def _hgrn_pivot(g, h, row):
    n, w = g.shape
    if h >= SUBLANES:
        parts = [jnp.broadcast_to(g[p:p + 1], (2 * h, w)) for p in range(h - 1, n, 2 * h)]
        return parts[0] if len(parts) == 1 else jnp.concatenate(parts, axis=0)
    if h == 1:
        return jnp.where((row & 1) != 0, pltpu.roll(g, 1, axis=0), g)
    parts = []
    row8 = lax.broadcasted_iota(jnp.int32, (SUBLANES, w), 0)
    for lo in range(0, n, SUBLANES):
        cands = [jnp.broadcast_to(g[p:p + 1], (SUBLANES, w)) for p in range(lo + h - 1, lo + SUBLANES, 2 * h)]
        piv = cands[-1]
        for j in range(len(cands) - 2, -1, -1):
            piv = jnp.where(row8 < (j + 1) * 2 * h, cands[j], piv)
        parts.append(piv)
    return jnp.concatenate(parts, axis=0)


def _hgrn_kernel(q_ref, f_ref, i_ref, go_ref, lbl_ref, ng_ref, o_ref, st_ref, *, seq):
    lbl = lbl_ref[...]
    e = jnp.exp(lbl - jnp.max(lbl, axis=0, keepdims=True))
    lb = e[0:1] / jnp.sum(e, axis=0, keepdims=True)
    ng = ng_ref[...]
    st_ref[...] = jnp.zeros_like(st_ref)

    row = lax.broadcasted_iota(jnp.int32, (CHUNK, HG_WIDTH), 0)
    row_a = lax.broadcasted_iota(jnp.int32, (CHUNK, CHUNK), 0)
    col_a = lax.broadcasted_iota(jnp.int32, (CHUNK, CHUNK), 1)
    nt = (((1,), (1,)), ((), ()))
    heads = [slice(hd * HG_D, (hd + 1) * HG_D) for hd in range(HG_HEADS)]

    def chunk(c, carry):
        rows = pl.ds(pl.multiple_of(c * CHUNK, CHUNK), CHUNK)
        f = lb + (1.0 - lb) * _sigmoid(f_ref[rows, :].astype(F32))
        k_all = 1.0 - f
        g_all = jnp.log(f) * LOG2E
        sh = 1
        while sh < CHUNK:
            g_all = g_all + jnp.where(row >= sh, pltpu.roll(g_all, sh, axis=0), 0.0)
            sh *= 2
        q_all = _silu(q_ref[rows, :].astype(F32))
        go_all = _silu(go_ref[rows, :].astype(F32))
        i_all = i_ref[rows, :]

        a = [None] * HG_HEADS
        for h in HG_LEVELS:
            d = jnp.exp2(-jnp.abs(g_all - _hgrn_pivot(g_all, h, row)))
            t = (jnp.where((row & h) != 0, q_all, k_all) * d).astype(BF16)
            keep = ((row_a // (2 * h)) == (col_a // (2 * h))) & ((row_a & h) != 0) & ((col_a & h) == 0)
            for hd, sl in enumerate(heads):
                prod = lax.dot_general(t[:, sl], t[:, sl], nt, preferred_element_type=F32)
                a[hd] = jnp.where(keep, prod, 0.0 if a[hd] is None else a[hd])
        qk = q_all * k_all
        for hd, sl in enumerate(heads):
            a[hd] = jnp.where(row_a == col_a, jnp.sum(qk[:, sl], axis=-1, keepdims=True), a[hd]).astype(BF16)

        g_last = g_all[CHUNK - 1:CHUNK]
        q_in = (q_all * jnp.exp2(g_all)).astype(BF16)
        k_up = (k_all * jnp.exp2(g_last - g_all)).astype(BF16)
        dec = jnp.exp2(g_last)
        for hd, sl in enumerate(heads):
            st = st_ref[hd]
            o = lax.dot_general(q_in[:, sl], st.astype(BF16), nt, preferred_element_type=F32)
            o = o + jnp.dot(a[hd], i_all[:, sl], preferred_element_type=F32)
            i_t = i_all[:, sl].astype(F32).T.astype(BF16)
            st_ref[hd] = st * dec[:, sl] + jnp.dot(i_t, k_up[:, sl], preferred_element_type=F32)
            o_ref[rows, sl] = (_rms_rows(o, ng) * go_all[:, sl]).astype(o_ref.dtype)
        return carry

    lax.fori_loop(0, seq // CHUNK, chunk, 0)


def _hgrn(zhg, lb_logits, norm_g, batch, seq):
    zhg3 = zhg.reshape(batch, seq, 4 * HG_WIDTH)
    blk = lambda off: pl.BlockSpec((None, seq, HG_WIDTH), lambda b: (b, 0, off))
    return pl.pallas_call(
        functools.partial(_hgrn_kernel, seq=seq),
        grid=(batch,),
        in_specs=[blk(0), blk(1), blk(2), blk(3), _const_spec(lb_logits.shape), _const_spec(norm_g.shape)],
        out_specs=pl.BlockSpec((None, seq, HG_WIDTH), lambda b: (b, 0, 0)),
        out_shape=jax.ShapeDtypeStruct((batch, seq, HG_WIDTH), BF16),
        scratch_shapes=[pltpu.VMEM((HG_HEADS, HG_D, HG_D), F32)],
        compiler_params=_params(("parallel",)),
        name="hgrn",
    )(zhg3, zhg3, zhg3, zhg3, lb_logits, norm_g)


FOX_AUG = FOX_DH
FOX_NAUG = 6


def _split3(x):
    hi = x.astype(BF16).astype(F32)
    r = x - hi
    lo = r.astype(BF16).astype(F32)
    return hi, lo, r - lo


def _foxprep_kernel(q_ref, k_ref, v_ref, ff_ref, fb_ref, qg_ref, kg_ref, bd_ref, place_ref,
                    qa_ref, ka_ref, vt_ref, *, seq):
    for p in range(FOX_HEADS // 2):
        vt_ref[p] = v_ref[:, p * LANES:(p + 1) * LANES].astype(F32).T.astype(BF16)
    x = ff_ref[...] + fb_ref[...]
    fc = -(jnp.maximum(-x, 0.0) + jnp.log1p(jnp.exp(-jnp.abs(x))))
    pos = lax.broadcasted_iota(jnp.int32, (seq, LANES), 0)
    sh = 1
    while sh < seq:
        fc = fc + jnp.where(pos >= sh, pltpu.roll(fc, sh, axis=0), 0.0)
        sh *= 2
    fc = fc * LOG2E

    def normed(ref, g_ref, scale):
        v = ref[...].astype(F32)
        sq = (v * v).astype(BF16)
        w = bd_ref.shape[0]
        ms = jnp.concatenate([jnp.dot(sq[:, c:c + w], bd_ref[...], preferred_element_type=F32)
                              for c in range(0, FOX_WIDTH, w)], axis=1)
        return v * lax.rsqrt(ms + EPS) * (g_ref[...] * scale)

    qn = normed(q_ref, qg_ref, FOX_DH ** -0.5 * LOG2E)
    kn = normed(k_ref, kg_ref, 1.0)
    lane = lax.broadcasted_iota(jnp.int32, (seq, LANES), 1)
    hi, lo, lo2 = _split3(fc)
    hi = jnp.where(lane == FOX_HEADS, 1.0, hi)
    aug = jnp.dot(jnp.concatenate([hi, lo, lo2], axis=1).astype(BF16), place_ref[...],
                  preferred_element_type=F32)
    for h in range(FOX_HEADS):
        aug_q = pltpu.roll(aug, (FOX_AUG - FOX_NAUG * h) % LANES, axis=1)
        aug_k = pltpu.roll(aug, (FOX_AUG - FOX_DH - FOX_NAUG * h) % LANES, axis=1) if h else aug
        pair = slice(LANES * (h // 2), LANES * (h // 2) + LANES)
        pq, pk = qn[:, pair], kn[:, pair]
        if h % 2:
            pq = pltpu.roll(pq, FOX_DH, axis=1)
            pk = pltpu.roll(pk, FOX_DH, axis=1)
        qa_ref[h] = jnp.where(lane < FOX_DH, pq, aug_q).astype(BF16)
        ka_ref[h] = jnp.where(lane < FOX_DH, pk, jnp.where(lane < FOX_AUG + FOX_NAUG, aug_k, 0.0)).astype(BF16)


def _fox_place_matrix():
    place = np.zeros((3 * LANES, LANES), np.float32)
    for h in range(FOX_HEADS):
        for part in range(3):
            place[part * LANES + h, FOX_NAUG * h + part] = 1.0
            place[part * LANES + h, FOX_DH + FOX_NAUG * h + 3 + part] = -1.0
            place[FOX_HEADS, FOX_NAUG * h + 3 + part] = 1.0
            place[FOX_HEADS, FOX_DH + FOX_NAUG * h + part] = 1.0
    return jnp.asarray(place, BF16)


def _fox_prep(zfox3, zff3, fb, qg, kg, bd, batch, seq):
    place = _fox_place_matrix()
    head_out = pl.BlockSpec((None, FOX_HEADS, seq, LANES), lambda b: (b, 0, 0, 0))
    return pl.pallas_call(
        functools.partial(_foxprep_kernel, seq=seq),
        grid=(batch,),
        in_specs=[pl.BlockSpec((None, seq, FOX_WIDTH), lambda b: (b, 0, 0)),
                  pl.BlockSpec((None, seq, FOX_WIDTH), lambda b: (b, 0, 1)),
                  pl.BlockSpec((None, seq, FOX_WIDTH), lambda b: (b, 0, 2)),
                  pl.BlockSpec((None, seq, LANES), lambda b: (b, 0, 0)),
                  _const_spec(fb.shape), _const_spec(qg.shape), _const_spec(kg.shape), _const_spec(bd.shape),
                  _const_spec(place.shape)],
        out_specs=[head_out, head_out,
                   pl.BlockSpec((None, FOX_HEADS // 2, LANES, seq), lambda b: (b, 0, 0, 0))],
        out_shape=[jax.ShapeDtypeStruct((batch, FOX_HEADS, seq, LANES), BF16)] * 2
                  + [jax.ShapeDtypeStruct((batch, FOX_HEADS // 2, LANES, seq), BF16)],
        compiler_params=_params(("parallel",)),
        name="fox_prep",
    )(zfox3, zfox3, zfox3, zff3, fb, qg, kg, bd, place)


def _foxattn_kernel(qa_ref, ka_ref, vt_ref, o_ref, s_ref, m_ref, l_ref, acc_ref, *, tq, n_blk):
    key = lax.broadcasted_iota(jnp.int32, (tq, tq), 0)
    qry = lax.broadcasted_iota(jnp.int32, (tq, tq), 1)
    nt = (((1,), (1,)), ((), ()))
    steps = [(i, j) for i in range(n_blk) for j in range(i + 1)]

    def scores(n, hh):
        i, j = steps[n]
        s_ref[n % 2, hh] = lax.dot_general(ka_ref[hh, j * tq:(j + 1) * tq, :], qa_ref[hh, i * tq:(i + 1) * tq, :],
                                           nt, preferred_element_type=F32)

    def reduce(n, hh):
        i, j = steps[n]
        s = s_ref[n % 2, hh]
        if j == i:
            s = jnp.where(key <= qry, s, NEG_BIG)
        m_new = jnp.max(s, axis=0, keepdims=True)
        vt = vt_ref[hh * FOX_DH:(hh + 1) * FOX_DH, j * tq:(j + 1) * tq]
        if j == 0:
            p = jnp.exp2(s - m_new)
            m_ref[hh] = m_new
            l_ref[hh] = jnp.sum(p, axis=0, keepdims=True)
            acc_ref[hh] = jnp.dot(vt, p.astype(BF16), preferred_element_type=F32)
        else:
            m = m_ref[hh]
            m_new = jnp.maximum(m, m_new)
            alpha = jnp.exp2(m - m_new)
            p = jnp.exp2(s - m_new)
            m_ref[hh] = m_new
            l_ref[hh] = alpha * l_ref[hh] + jnp.sum(p, axis=0, keepdims=True)
            acc_ref[hh] = alpha * acc_ref[hh] + jnp.dot(vt, p.astype(BF16), preferred_element_type=F32)

    for hh in range(2):
        scores(0, hh)
    for n, (i, j) in enumerate(steps):
        for hh in range(2):
            if n + 1 < len(steps):
                scores(n + 1, hh)
            reduce(n, hh)
        if j == i:
            out_t = jnp.concatenate([acc_ref[hh] / l_ref[hh] for hh in range(2)], axis=0)
            o_ref[i * tq:(i + 1) * tq, :] = out_t.T.astype(o_ref.dtype)


def _fox_attn(qa, ka, vt, batch, seq, tq):
    pair = lambda b, p: (b, p, 0, 0)
    return pl.pallas_call(
        functools.partial(_foxattn_kernel, tq=tq, n_blk=seq // tq),
        grid=(batch, FOX_HEADS // 2),
        in_specs=[pl.BlockSpec((None, 2, seq, LANES), pair), pl.BlockSpec((None, 2, seq, LANES), pair),
                  pl.BlockSpec((None, None, LANES, seq), pair)],
        out_specs=pl.BlockSpec((None, seq, LANES), lambda b, p: (b, 0, p)),
        out_shape=jax.ShapeDtypeStruct((batch, seq, FOX_WIDTH), BF16),
        scratch_shapes=[pltpu.VMEM((2, 2, tq, tq), F32), pltpu.VMEM((2, 1, tq), F32), pltpu.VMEM((2, 1, tq), F32),
                        pltpu.VMEM((2, FOX_DH, tq), F32)],
        compiler_params=_params(("parallel", "parallel")),
        name="fox_attn",
    )(qa, ka, vt)


def _memkv_kernel(m_ref, g_ref, w_ref, kg_ref, k_ref, v_ref):
    h = _rms_rows(m_ref[...], g_ref[...]).astype(BF16)
    kv = jnp.dot(h, w_ref[...], preferred_element_type=F32)
    for hd in range(MEM_HEADS):
        sl = slice(hd * MEM_DH, (hd + 1) * MEM_DH)
        k_ref[:, sl] = _rms_rows(kv[:, sl], kg_ref[...]).astype(k_ref.dtype)
    v_ref[...] = kv[:, MEM_WIDTH:].astype(v_ref.dtype)


def _mem_kv(mem2, g, w, kg, tm):
    n_rows = mem2.shape[0]
    row = lambda i: (i, 0)
    return pl.pallas_call(
        _memkv_kernel,
        grid=(n_rows // tm,),
        in_specs=[pl.BlockSpec((tm, D_MODEL), row), _const_spec(g.shape), _const_spec(w.shape),
                  _const_spec(kg.shape)],
        out_specs=[pl.BlockSpec((tm, MEM_WIDTH), row)] * 2,
        out_shape=[jax.ShapeDtypeStruct((n_rows, MEM_WIDTH), BF16)] * 2,
        compiler_params=_params(("parallel",)),
        name="mem_kv",
    )(mem2, g, w, kg)


def _memattn_kernel(q_ref, k_ref, v_ref, qg_ref, o_ref):
    for hd in range(MEM_HEADS):
        sl = slice(hd * MEM_DH, (hd + 1) * MEM_DH)
        q = _rms_rows(q_ref[:, sl].astype(F32), qg_ref[...] * (MEM_DH ** -0.5)).astype(BF16)
        s = lax.dot_general(q, k_ref[:, sl], (((1,), (1,)), ((), ())), preferred_element_type=F32)
        p = jnp.exp(s - jnp.max(s, axis=-1, keepdims=True))
        l = jnp.sum(p, axis=-1, keepdims=True)
        o = jnp.dot(p.astype(BF16), v_ref[:, sl], preferred_element_type=F32)
        o_ref[:, sl] = (o / l).astype(o_ref.dtype)


def _mem_attn(zmq3, mk3, mv3, qg, batch, seq, tq):
    n_mem = mk3.shape[1]
    kv_spec = pl.BlockSpec((None, n_mem, MEM_WIDTH), lambda b, i: (b, 0, 0))
    return pl.pallas_call(
        _memattn_kernel,
        grid=(batch, seq // tq),
        in_specs=[pl.BlockSpec((None, tq, MEM_WIDTH), lambda b, i: (b, i, 0)), kv_spec, kv_spec,
                  _const_spec(qg.shape)],
        out_specs=pl.BlockSpec((None, tq, MEM_WIDTH), lambda b, i: (b, i, 0)),
        out_shape=jax.ShapeDtypeStruct((batch, seq, MEM_WIDTH), BF16),
        compiler_params=_params(("parallel", "parallel")),
        name="mem_attn",
    )(zmq3, mk3, mv3, qg)


def _mergeffn_kernel(x_ref, ya_ref, yb_ref, yc_ref, gt_ref, wba_ref, wbb_ref, wbc_ref, wo_ref, ng_ref,
                     wa_ref, wv_ref, cw_ref, cb_ref, wd_ref, o_ref, carry_ref, y_ref, *, tm, fc):
    @pl.when(pl.program_id(1) == 0)
    def _():
        carry_ref[...] = jnp.zeros_like(carry_ref)

    merged = None
    for br, (yb_r, w_ref) in enumerate(((ya_ref, wba_ref), (yb_ref, wbb_ref), (yc_ref, wbc_ref))):
        gate = _sigmoid(gt_ref[:, br * D_MODEL:(br + 1) * D_MODEL].astype(F32))
        t = gate * jnp.dot(yb_r[...], w_ref[...], preferred_element_type=F32)
        merged = t if merged is None else merged + t
    x1 = x_ref[...] + jnp.dot(merged.astype(BF16), wo_ref[...], preferred_element_type=F32)
    o_ref[...] = x1

    h = _rms_rows(x1, ng_ref[...]).astype(BF16)
    row = lax.broadcasted_iota(jnp.int32, (tm, fc), 0)
    for c in range(D_FF // fc):
        sl = slice(c * fc, (c + 1) * fc)
        a = jnp.dot(h, wa_ref[:, sl], preferred_element_type=F32)
        v = jnp.dot(h, wv_ref[:, sl], preferred_element_type=F32)
        prev = carry_ref[c]
        p1 = prev[SUBLANES - 1:SUBLANES]
        p2 = prev[SUBLANES - 2:SUBLANES - 1]
        a1 = jnp.where(row == 0, p1, pltpu.roll(a, 1, axis=0))
        a2 = jnp.where(row == 0, p2, jnp.where(row == 1, p1, pltpu.roll(a, 2, axis=0)))
        carry_ref[c] = a[tm - SUBLANES:tm]
        z = a2 * cw_ref[0:1, sl] + a1 * cw_ref[1:2, sl] + a * cw_ref[2:3, sl] + cb_ref[:, sl]
        gelu = 0.5 * z * (1.0 + lax.erf(z * (2.0 ** -0.5)))
        y_ref[:, sl] = (gelu * v).astype(y_ref.dtype)
    o_ref[...] += jnp.dot(y_ref[...], wd_ref[...], preferred_element_type=F32)


def _merge_ffn(x3, ya, yb, yc, zgate3, wba, wbb, wbc, wo, ng, wa, wv, cw, cb, wd, tm, fc):
    batch, seq, _ = x3.shape
    tile = lambda w: pl.BlockSpec((None, tm, w), lambda b, t: (b, t, 0))
    consts = (wba, wbb, wbc, wo, ng, wa, wv, cw, cb, wd)
    return pl.pallas_call(
        functools.partial(_mergeffn_kernel, tm=tm, fc=fc),
        grid=(batch, seq // tm),
        in_specs=[tile(D_MODEL), tile(HG_WIDTH), tile(FOX_WIDTH), tile(MEM_WIDTH), tile(N_BRANCH * D_MODEL)]
                 + [_const_spec(c.shape) for c in consts],
        out_specs=tile(D_MODEL),
        out_shape=jax.ShapeDtypeStruct((batch, seq, D_MODEL), F32),
        scratch_shapes=[pltpu.VMEM((D_FF // fc, SUBLANES, fc), F32), pltpu.VMEM((tm, D_FF), BF16)],
        compiler_params=_params(("parallel", "arbitrary")),
        name="merge_ffn",
    )(x3, ya, yb, yc, zgate3, *consts)


def kernel(x, mem, norm_mix_g, norm_mem_g, w_in, hgrn_lb_logits, hgrn_norm_g, fox_f_bias, fox_q_norm_g,
           fox_k_norm_g, mem_kv_w, mem_q_norm_g, mem_k_norm_g, w_br_hgrn, w_br_fox, w_br_mem, w_out,
           norm_ffn_g, ffn_w_up, ffn_conv_w, ffn_conv_b, ffn_w_down):
    batch, seq, _ = x.shape
    n_mem = mem.shape[1]
    depth = w_in.shape[0]
    assert depth == 1 and seq % 512 == 0

    x2 = x.reshape(batch * seq, D_MODEL)
    for l in range(depth):
        w = w_in[l]
        c_hg, c_fox = 4 * HG_WIDTH, 3 * FOX_WIDTH
        o_ff = c_hg + c_fox
        o_mq = o_ff + FOX_HEADS
        o_gate = o_mq + MEM_WIDTH
        whg = w[:, :c_hg].astype(BF16)
        wfox = w[:, c_hg:o_ff].astype(BF16)
        wff = jnp.pad(w[:, o_ff:o_mq], ((0, 0), (0, LANES - FOX_HEADS))).astype(BF16)
        wmq = w[:, o_mq:o_gate].astype(BF16)
        wgate = w[:, o_gate:].astype(BF16)
        row = lambda v: v.reshape(1, -1).astype(F32)

        zhg, zfox, zmq, zgate, zff = _in_proj(x2, row(norm_mix_g[l]), whg, wfox, wmq, wgate, wff, tm=512)

        ya = _hgrn(zhg, hgrn_lb_logits.astype(F32), row(hgrn_norm_g[l]), batch, seq)

        zfox3 = zfox.reshape(batch, seq, c_fox)
        fb = jnp.pad(row(fox_f_bias[l]), ((0, 0), (0, LANES - FOX_HEADS)))
        head_id = jnp.arange(MXU_WIDTH) // FOX_DH
        bd = (head_id[:, None] == head_id[None, :]).astype(BF16) * (1.0 / FOX_DH)
        qa, ka, vt = _fox_prep(zfox3, zff.reshape(batch, seq, LANES), fb,
                               jnp.tile(row(fox_q_norm_g[l]), (1, FOX_HEADS)),
                               jnp.tile(row(fox_k_norm_g[l]), (1, FOX_HEADS)), bd, batch, seq)
        yb = _fox_attn(qa, ka, vt, batch, seq, tq=512)

        mk, mv = _mem_kv(mem.reshape(batch * n_mem, D_MODEL), row(norm_mem_g[l]), mem_kv_w[l].astype(BF16),
                         row(mem_k_norm_g[l]), tm=512)
        yc = _mem_attn(zmq.reshape(batch, seq, MEM_WIDTH), mk.reshape(batch, n_mem, MEM_WIDTH),
                       mv.reshape(batch, n_mem, MEM_WIDTH), row(mem_q_norm_g[l]), batch, seq, tq=512)

        wup = ffn_w_up[l]
        out = _merge_ffn(x2.reshape(batch, seq, D_MODEL), ya, yb, yc, zgate.reshape(batch, seq, N_BRANCH * D_MODEL),
                         w_br_hgrn[l].astype(BF16), w_br_fox[l].astype(BF16), w_br_mem[l].astype(BF16),
                         w_out[l].astype(BF16), row(norm_ffn_g[l]), wup[:, :D_FF].astype(BF16),
                         wup[:, D_FF:].astype(BF16), ffn_conv_w[l].astype(F32), row(ffn_conv_b[l]),
                         ffn_w_down[l].astype(BF16), tm=512, fc=256)
        x2 = out.reshape(batch * seq, D_MODEL)
    return x2.reshape(batch, seq, D_MODEL)
```

```python
import functools

import numpy as np
import jax
import jax.numpy as jnp
from jax import lax
from jax.experimental import pallas as pl
from jax.experimental.pallas import tpu as pltpu

F32 = jnp.float32
BF16 = jnp.bfloat16

D_MODEL = 1024
EPS = 1e-6
CHUNK = 64
SUB = 16
HG_HEADS = 4
HG_D = 128
HG_WIDTH = HG_HEADS * HG_D
FOX_HEADS = 8
FOX_DH = 64
FOX_WIDTH = FOX_HEADS * FOX_DH
MEM_HEADS = 4
MEM_DH = 128
MEM_WIDTH = MEM_HEADS * MEM_DH
N_BRANCH = 3
D_FF = 2816
CONV_W = 3

LANES = 128
SUBLANES = 8
MXU_WIDTH = 256
VMEM_LIMIT = 56 * 1024 * 1024

NEG_BIG = -1e30


def _const_spec(shape):
    nd = len(shape)
    return pl.BlockSpec(shape, lambda *_: (0,) * nd, pipeline_mode=pl.Buffered(1))


def _params(sem):
    return pltpu.CompilerParams(dimension_semantics=sem, vmem_limit_bytes=VMEM_LIMIT)


def _rms_rows(x, g):
    return x * lax.rsqrt(jnp.mean(x * x, axis=-1, keepdims=True) + EPS) * g


def _sigmoid(x):
    return 1.0 / (1.0 + jnp.exp(-x))


def _silu(x):
    return x * _sigmoid(x)


def _inproj_kernel(x_ref, g_ref, whg_ref, wfox_ref, wmq_ref, wgate_ref, wff_ref,
                   zhg_ref, zfox_ref, zmq_ref, zgate_ref, zff_ref, *, tn):
    h = _rms_rows(x_ref[...], g_ref[...]).astype(BF16)
    for w_ref, o_ref in ((whg_ref, zhg_ref), (wfox_ref, zfox_ref), (wmq_ref, zmq_ref),
                         (wgate_ref, zgate_ref), (wff_ref, zff_ref)):
        n = w_ref.shape[1]
        step = min(tn, n)
        for c in range(0, n, step):
            o_ref[:, c:c + step] = jnp.dot(h, w_ref[:, c:c + step],
                                           preferred_element_type=F32).astype(o_ref.dtype)


def _in_proj(x2, g, whg, wfox, wmq, wgate, wff, tm):
    n_rows = x2.shape[0]
    ws = (whg, wfox, wmq, wgate, wff)
    out_dtypes = (BF16, BF16, BF16, BF16, F32)
    row = lambda i: (i, 0)
    return pl.pallas_call(
        functools.partial(_inproj_kernel, tn=512),
        grid=(n_rows // tm,),
        in_specs=[pl.BlockSpec((tm, D_MODEL), row), _const_spec(g.shape)]
                 + [_const_spec(w.shape) for w in ws],
        out_specs=[pl.BlockSpec((tm, w.shape[1]), row) for w in ws],
        out_shape=[jax.ShapeDtypeStruct((n_rows, w.shape[1]), dt) for w, dt in zip(ws, out_dtypes)],
        compiler_params=_params(("parallel",)),
        name="in_proj",
    )(x2, g, *ws)


HG_LEVELS = (32, 16, 8, 4, 2, 1)
LOG2E = 1.4426950408889634


def _hgrn_pivot(g, h, row):
    n, w = g.shape
    if h >= SUBLANES:
        parts = [jnp.broadcast_to(g[p:p + 1], (2 * h, w)) for p in range(h - 1, n, 2 * h)]
        return parts[0] if len(parts) == 1 else jnp.concatenate(parts, axis=0)
    if h == 1:
        return jnp.where((row & 1) != 0, pltpu.roll(g, 1, axis=0), g)
    parts = []
    row8 = lax.broadcasted_iota(jnp.int32, (SUBLANES, w), 0)
    for lo in range(0, n, SUBLANES):
        cands = [jnp.broadcast_to(g[p:p + 1], (SUBLANES, w)) for p in range(lo + h - 1, lo + SUBLANES, 2 * h)]
        piv = cands[-1]
        for j in range(len(cands) - 2, -1, -1):
            piv = jnp.where(row8 < (j + 1) * 2 * h, cands[j], piv)
        parts.append(piv)
    return jnp.concatenate(parts, axis=0)


def _hgrn_kernel(q_ref, f_ref, i_ref, go_ref, lbl_ref, ng_ref, o_ref, st_ref, *, seq):
    lbl = lbl_ref[...]
    e = jnp.exp(lbl - jnp.max(lbl, axis=0, keepdims=True))
    lb = e[0:1] / jnp.sum(e, axis=0, keepdims=True)
    ng = ng_ref[...]
    st_ref[...] = jnp.zeros_like(st_ref)

    row = lax.broadcasted_iota(jnp.int32, (CHUNK, HG_WIDTH), 0)
    row_a = lax.broadcasted_iota(jnp.int32, (CHUNK, CHUNK), 0)
    col_a = lax.broadcasted_iota(jnp.int32, (CHUNK, CHUNK), 1)
    nt = (((1,), (1,)), ((), ()))
    heads = [slice(hd * HG_D, (hd + 1) * HG_D) for hd in range(HG_HEADS)]

    def chunk(c, carry):
        rows = pl.ds(pl.multiple_of(c * CHUNK, CHUNK), CHUNK)
        f = lb + (1.0 - lb) * _sigmoid(f_ref[rows, :].astype(F32))
        k_all = 1.0 - f
        g_all = jnp.log(f) * LOG2E
        sh = 1
        while sh < CHUNK:
            g_all = g_all + jnp.where(row >= sh, pltpu.roll(g_all, sh, axis=0), 0.0)
            sh *= 2
        q_all = _silu(q_ref[rows, :].astype(F32))
        go_all = _silu(go_ref[rows, :].astype(F32))
        i_all = i_ref[rows, :]

        a = [None] * HG_HEADS
        for h in HG_LEVELS:
            d = jnp.exp2(-jnp.abs(g_all - _hgrn_pivot(g_all, h, row)))
            t = (jnp.where((row & h) != 0, q_all, k_all) * d).astype(BF16)
            keep = ((row_a // (2 * h)) == (col_a // (2 * h))) & ((row_a & h) != 0) & ((col_a & h) == 0)
            for hd, sl in enumerate(heads):
                prod = lax.dot_general(t[:, sl], t[:, sl], nt, preferred_element_type=F32)
                a[hd] = jnp.where(keep, prod, 0.0 if a[hd] is None else a[hd])
        qk = q_all * k_all
        for hd, sl in enumerate(heads):
            a[hd] = jnp.where(row_a == col_a, jnp.sum(qk[:, sl], axis=-1, keepdims=True), a[hd]).astype(BF16)

        g_last = g_all[CHUNK - 1:CHUNK]
        q_in = (q_all * jnp.exp2(g_all)).astype(BF16)
        k_up = (k_all * jnp.exp2(g_last - g_all)).astype(BF16)
        dec = jnp.exp2(g_last)
        for hd, sl in enumerate(heads):
            st = st_ref[hd]
            o = lax.dot_general(q_in[:, sl], st.astype(BF16), nt, preferred_element_type=F32)
            o = o + jnp.dot(a[hd], i_all[:, sl], preferred_element_type=F32)
            i_t = i_all[:, sl].astype(F32).T.astype(BF16)
            st_ref[hd] = st * dec[:, sl] + jnp.dot(i_t, k_up[:, sl], preferred_element_type=F32)
            o_ref[rows, sl] = (_rms_rows(o, ng) * go_all[:, sl]).astype(o_ref.dtype)
        return carry

    lax.fori_loop(0, seq // CHUNK, chunk, 0)


def _hgrn(zhg, lb_logits, norm_g, batch, seq):
    zhg3 = zhg.reshape(batch, seq, 4 * HG_WIDTH)
    blk = lambda off: pl.BlockSpec((None, seq, HG_WIDTH), lambda b: (b, 0, off))
    return pl.pallas_call(
        functools.partial(_hgrn_kernel, seq=seq),
        grid=(batch,),
        in_specs=[blk(0), blk(1), blk(2), blk(3), _const_spec(lb_logits.shape), _const_spec(norm_g.shape)],
        out_specs=pl.BlockSpec((None, seq, HG_WIDTH), lambda b: (b, 0, 0)),
        out_shape=jax.ShapeDtypeStruct((batch, seq, HG_WIDTH), BF16),
        scratch_shapes=[pltpu.VMEM((HG_HEADS, HG_D, HG_D), F32)],
        compiler_params=_params(("parallel",)),
        name="hgrn",
    )(zhg3, zhg3, zhg3, zhg3, lb_logits, norm_g)


FOX_AUG = FOX_DH
FOX_NAUG = 6


def _split3(x):
    hi = x.astype(BF16).astype(F32)
    r = x - hi
    lo = r.astype(BF16).astype(F32)
    return hi, lo, r - lo


def _foxprep_kernel(q_ref, k_ref, v_ref, ff_ref, fb_ref, qg_ref, kg_ref, bd_ref, place_ref,
                    qa_ref, ka_ref, vt_ref, *, seq):
    for p in range(FOX_HEADS // 2):
        vt_ref[p] = v_ref[:, p * LANES:(p + 1) * LANES].astype(F32).T.astype(BF16)
    x = ff_ref[...] + fb_ref[...]
    fc = -(jnp.maximum(-x, 0.0) + jnp.log1p(jnp.exp(-jnp.abs(x))))
    pos = lax.broadcasted_iota(jnp.int32, (seq, LANES), 0)
    sh = 1
    while sh < seq:
        fc = fc + jnp.where(pos >= sh, pltpu.roll(fc, sh, axis=0), 0.0)
        sh *= 2
    fc = fc * LOG2E

    def normed(ref, g_ref, scale):
        v = ref[...].astype(F32)
        sq = (v * v).astype(BF16)
        w = bd_ref.shape[0]
        ms = jnp.concatenate([jnp.dot(sq[:, c:c + w], bd_ref[...], preferred_element_type=F32)
                              for c in range(0, FOX_WIDTH, w)], axis=1)
        return v * lax.rsqrt(ms + EPS) * (g_ref[...] * scale)

    qn = normed(q_ref, qg_ref, FOX_DH ** -0.5 * LOG2E)
    kn = normed(k_ref, kg_ref, 1.0)
    lane = lax.broadcasted_iota(jnp.int32, (seq, LANES), 1)
    hi, lo, lo2 = _split3(fc)
    hi = jnp.where(lane == FOX_HEADS, 1.0, hi)
    aug = jnp.dot(jnp.concatenate([hi, lo, lo2], axis=1).astype(BF16), place_ref[...],
                  preferred_element_type=F32)
    for h in range(FOX_HEADS):
        aug_q = pltpu.roll(aug, (FOX_AUG - FOX_NAUG * h) % LANES, axis=1)
        aug_k = pltpu.roll(aug, (FOX_AUG - FOX_DH - FOX_NAUG * h) % LANES, axis=1) if h else aug
        pair = slice(LANES * (h // 2), LANES * (h // 2) + LANES)
        pq, pk = qn[:, pair], kn[:, pair]
        if h % 2:
            pq = pltpu.roll(pq, FOX_DH, axis=1)
            pk = pltpu.roll(pk, FOX_DH, axis=1)
        qa_ref[h] = jnp.where(lane < FOX_DH, pq, aug_q).astype(BF16)
        ka_ref[h] = jnp.where(lane < FOX_DH, pk, jnp.where(lane < FOX_AUG + FOX_NAUG, aug_k, 0.0)).astype(BF16)


def _fox_place_matrix():
    place = np.zeros((3 * LANES, LANES), np.float32)
    for h in range(FOX_HEADS):
        for part in range(3):
            place[part * LANES + h, FOX_NAUG * h + part] = 1.0
            place[part * LANES + h, FOX_DH + FOX_NAUG * h + 3 + part] = -1.0
            place[FOX_HEADS, FOX_NAUG * h + 3 + part] = 1.0
            place[FOX_HEADS, FOX_DH + FOX_NAUG * h + part] = 1.0
    return jnp.asarray(place, BF16)


def _fox_prep(zfox3, zff3, fb, qg, kg, bd, batch, seq):
    place = _fox_place_matrix()
    head_out = pl.BlockSpec((None, FOX_HEADS, seq, LANES), lambda b: (b, 0, 0, 0))
    return pl.pallas_call(
        functools.partial(_foxprep_kernel, seq=seq),
        grid=(batch,),
        in_specs=[pl.BlockSpec((None, seq, FOX_WIDTH), lambda b: (b, 0, 0)),
                  pl.BlockSpec((None, seq, FOX_WIDTH), lambda b: (b, 0, 1)),
                  pl.BlockSpec((None, seq, FOX_WIDTH), lambda b: (b, 0, 2)),
                  pl.BlockSpec((None, seq, LANES), lambda b: (b, 0, 0)),
                  _const_spec(fb.shape), _const_spec(qg.shape), _const_spec(kg.shape), _const_spec(bd.shape),
                  _const_spec(place.shape)],
        out_specs=[head_out, head_out,
                   pl.BlockSpec((None, FOX_HEADS // 2, LANES, seq), lambda b: (b, 0, 0, 0))],
        out_shape=[jax.ShapeDtypeStruct((batch, FOX_HEADS, seq, LANES), BF16)] * 2
                  + [jax.ShapeDtypeStruct((batch, FOX_HEADS // 2, LANES, seq), BF16)],
        compiler_params=_params(("parallel",)),
        name="fox_prep",
    )(zfox3, zfox3, zfox3, zff3, fb, qg, kg, bd, place)


FOX_FAST_LOGIT_BOUND = 100.0


def _fox_fast(qa_ref, ka_ref, vt_ref, o_ref, tq, n_blk):
    key = lax.broadcasted_iota(jnp.int32, (tq, tq), 0)
    qry = lax.broadcasted_iota(jnp.int32, (tq, tq), 1)
    nt = (((1,), (1,)), ((), ()))
    for i in range(n_blk):
        l = [None, None]
        acc = [None, None]
        for j in range(i + 1):
            ss = [lax.dot_general(ka_ref[hh, j * tq:(j + 1) * tq, :], qa_ref[hh, i * tq:(i + 1) * tq, :],
                                  nt, preferred_element_type=F32) for hh in range(2)]
            for hh in range(2):
                s = jnp.where(key <= qry, ss[hh], NEG_BIG) if j == i else ss[hh]
                p = jnp.exp2(s)
                vt = vt_ref[hh * FOX_DH:(hh + 1) * FOX_DH, j * tq:(j + 1) * tq]
                ps = jnp.sum(p, axis=0, keepdims=True)
                pv = jnp.dot(vt, p.astype(BF16), preferred_element_type=F32)
                l[hh] = ps if j == 0 else l[hh] + ps
                acc[hh] = pv if j == 0 else acc[hh] + pv
        out_t = jnp.concatenate([acc[hh] / l[hh] for hh in range(2)], axis=0)
        o_ref[i * tq:(i + 1) * tq, :] = out_t.T.astype(o_ref.dtype)


def _foxattn_kernel(bound_ref, qa_ref, ka_ref, vt_ref, o_ref, s_ref, m_ref, l_ref, acc_ref, *, tq, n_blk):
    fast = bound_ref[0] <= FOX_FAST_LOGIT_BOUND

    @pl.when(fast)
    def _():
        _fox_fast(qa_ref, ka_ref, vt_ref, o_ref, tq, n_blk)

    @pl.when(jnp.logical_not(fast))
    def _():
        _fox_online(qa_ref, ka_ref, vt_ref, o_ref, s_ref, m_ref, l_ref, acc_ref, tq, n_blk)


def _fox_online(qa_ref, ka_ref, vt_ref, o_ref, s_ref, m_ref, l_ref, acc_ref, tq, n_blk):
    key = lax.broadcasted_iota(jnp.int32, (tq, tq), 0)
    qry = lax.broadcasted_iota(jnp.int32, (tq, tq), 1)
    nt = (((1,), (1,)), ((), ()))
    steps = [(i, j) for i in range(n_blk) for j in range(i + 1)]

    def scores(n, hh):
        i, j = steps[n]
        s_ref[n % 2, hh] = lax.dot_general(ka_ref[hh, j * tq:(j + 1) * tq, :], qa_ref[hh, i * tq:(i + 1) * tq, :],
                                           nt, preferred_element_type=F32)

    def reduce(n, hh):
        i, j = steps[n]
        s = s_ref[n % 2, hh]
        if j == i:
            s = jnp.where(key <= qry, s, NEG_BIG)
        m_new = jnp.max(s, axis=0, keepdims=True)
        vt = vt_ref[hh * FOX_DH:(hh + 1) * FOX_DH, j * tq:(j + 1) * tq]
        if j == 0:
            p = jnp.exp2(s - m_new)
            m_ref[hh] = m_new
            l_ref[hh] = jnp.sum(p, axis=0, keepdims=True)
            acc_ref[hh] = jnp.dot(vt, p.astype(BF16), preferred_element_type=F32)
        else:
            m = m_ref[hh]
            m_new = jnp.maximum(m, m_new)
            alpha = jnp.exp2(m - m_new)
            p = jnp.exp2(s - m_new)
            m_ref[hh] = m_new
            l_ref[hh] = alpha * l_ref[hh] + jnp.sum(p, axis=0, keepdims=True)
            acc_ref[hh] = alpha * acc_ref[hh] + jnp.dot(vt, p.astype(BF16), preferred_element_type=F32)

    for hh in range(2):
        scores(0, hh)
    for n, (i, j) in enumerate(steps):
        for hh in range(2):
            if n + 1 < len(steps):
                scores(n + 1, hh)
            reduce(n, hh)
        if j == i:
            out_t = jnp.concatenate([acc_ref[hh] / l_ref[hh] for hh in range(2)], axis=0)
            o_ref[i * tq:(i + 1) * tq, :] = out_t.T.astype(o_ref.dtype)


def _fox_attn(qk_bound, qa, ka, vt, batch, seq, tq):
    pair = lambda b, p: (b, p, 0, 0)
    return pl.pallas_call(
        functools.partial(_foxattn_kernel, tq=tq, n_blk=seq // tq),
        grid=(batch, FOX_HEADS // 2),
        in_specs=[pl.BlockSpec(memory_space=pltpu.SMEM),
                  pl.BlockSpec((None, 2, seq, LANES), pair), pl.BlockSpec((None, 2, seq, LANES), pair),
                  pl.BlockSpec((None, None, LANES, seq), pair)],
        out_specs=pl.BlockSpec((None, seq, LANES), lambda b, p: (b, 0, p)),
        out_shape=jax.ShapeDtypeStruct((batch, seq, FOX_WIDTH), BF16),
        scratch_shapes=[pltpu.VMEM((2, 2, tq, tq), F32), pltpu.VMEM((2, 1, tq), F32), pltpu.VMEM((2, 1, tq), F32),
                        pltpu.VMEM((2, FOX_DH, tq), F32)],
        compiler_params=_params(("parallel", "parallel")),
        name="fox_attn",
    )(qk_bound, qa, ka, vt)


def _memkv_kernel(m_ref, g_ref, w_ref, kg_ref, k_ref, v_ref):
    h = _rms_rows(m_ref[...], g_ref[...]).astype(BF16)
    kv = jnp.dot(h, w_ref[...], preferred_element_type=F32)
    for hd in range(MEM_HEADS):
        sl = slice(hd * MEM_DH, (hd + 1) * MEM_DH)
        k_ref[:, sl] = _rms_rows(kv[:, sl], kg_ref[...]).astype(k_ref.dtype)
    v_ref[...] = kv[:, MEM_WIDTH:].astype(v_ref.dtype)


def _mem_kv(mem2, g, w, kg, tm):
    n_rows = mem2.shape[0]
    row = lambda i: (i, 0)
    return pl.pallas_call(
        _memkv_kernel,
        grid=(n_rows // tm,),
        in_specs=[pl.BlockSpec((tm, D_MODEL), row), _const_spec(g.shape), _const_spec(w.shape),
                  _const_spec(kg.shape)],
        out_specs=[pl.BlockSpec((tm, MEM_WIDTH), row)] * 2,
        out_shape=[jax.ShapeDtypeStruct((n_rows, MEM_WIDTH), BF16)] * 2,
        compiler_params=_params(("parallel",)),
        name="mem_kv",
    )(mem2, g, w, kg)


def _memattn_kernel(q_ref, k_ref, v_ref, qg_ref, o_ref):
    for hd in range(MEM_HEADS):
        sl = slice(hd * MEM_DH, (hd + 1) * MEM_DH)
        q = _rms_rows(q_ref[:, sl].astype(F32), qg_ref[...] * (MEM_DH ** -0.5)).astype(BF16)
        s = lax.dot_general(q, k_ref[:, sl], (((1,), (1,)), ((), ())), preferred_element_type=F32)
        p = jnp.exp(s - jnp.max(s, axis=-1, keepdims=True))
        l = jnp.sum(p, axis=-1, keepdims=True)
        o = jnp.dot(p.astype(BF16), v_ref[:, sl], preferred_element_type=F32)
        o_ref[:, sl] = (o / l).astype(o_ref.dtype)


def _mem_attn(zmq3, mk3, mv3, qg, batch, seq, tq):
    n_mem = mk3.shape[1]
    kv_spec = pl.BlockSpec((None, n_mem, MEM_WIDTH), lambda b, i: (b, 0, 0))
    return pl.pallas_call(
        _memattn_kernel,
        grid=(batch, seq // tq),
        in_specs=[pl.BlockSpec((None, tq, MEM_WIDTH), lambda b, i: (b, i, 0)), kv_spec, kv_spec,
                  _const_spec(qg.shape)],
        out_specs=pl.BlockSpec((None, tq, MEM_WIDTH), lambda b, i: (b, i, 0)),
        out_shape=jax.ShapeDtypeStruct((batch, seq, MEM_WIDTH), BF16),
        compiler_params=_params(("parallel", "parallel")),
        name="mem_attn",
    )(zmq3, mk3, mv3, qg)


def _mergeffn_kernel(x_ref, ya_ref, yb_ref, yc_ref, gt_ref, wba_ref, wbb_ref, wbc_ref, wo_ref, ng_ref,
                     wa_ref, wv_ref, cw_ref, cb_ref, wd_ref, o_ref, carry_ref, y_ref, *, tm, fc):
    @pl.when(pl.program_id(1) == 0)
    def _():
        carry_ref[...] = jnp.zeros_like(carry_ref)

    merged = None
    for br, (yb_r, w_ref) in enumerate(((ya_ref, wba_ref), (yb_ref, wbb_ref), (yc_ref, wbc_ref))):
        gate = _sigmoid(gt_ref[:, br * D_MODEL:(br + 1) * D_MODEL].astype(F32))
        t = gate * jnp.dot(yb_r[...], w_ref[...], preferred_element_type=F32)
        merged = t if merged is None else merged + t
    x1 = x_ref[...] + jnp.dot(merged.astype(BF16), wo_ref[...], preferred_element_type=F32)
    o_ref[...] = x1

    h = _rms_rows(x1, ng_ref[...]).astype(BF16)
    row = lax.broadcasted_iota(jnp.int32, (tm, fc), 0)
    for c in range(D_FF // fc):
        sl = slice(c * fc, (c + 1) * fc)
        a = jnp.dot(h, wa_ref[:, sl], preferred_element_type=F32)
        v = jnp.dot(h, wv_ref[:, sl], preferred_element_type=F32)
        prev = carry_ref[c]
        p1 = prev[SUBLANES - 1:SUBLANES]
        p2 = prev[SUBLANES - 2:SUBLANES - 1]
        a1 = jnp.where(row == 0, p1, pltpu.roll(a, 1, axis=0))
        a2 = jnp.where(row == 0, p2, jnp.where(row == 1, p1, pltpu.roll(a, 2, axis=0)))
        carry_ref[c] = a[tm - SUBLANES:tm]
        z = a2 * cw_ref[0:1, sl] + a1 * cw_ref[1:2, sl] + a * cw_ref[2:3, sl] + cb_ref[:, sl]
        gelu = 0.5 * z * (1.0 + lax.erf(z * (2.0 ** -0.5)))
        y_ref[:, sl] = (gelu * v).astype(y_ref.dtype)
    o_ref[...] += jnp.dot(y_ref[...], wd_ref[...], preferred_element_type=F32)


def _merge_ffn(x3, ya, yb, yc, zgate3, wba, wbb, wbc, wo, ng, wa, wv, cw, cb, wd, tm, fc):
    batch, seq, _ = x3.shape
    tile = lambda w: pl.BlockSpec((None, tm, w), lambda b, t: (b, t, 0))
    consts = (wba, wbb, wbc, wo, ng, wa, wv, cw, cb, wd)
    return pl.pallas_call(
        functools.partial(_mergeffn_kernel, tm=tm, fc=fc),
        grid=(batch, seq // tm),
        in_specs=[tile(D_MODEL), tile(HG_WIDTH), tile(FOX_WIDTH), tile(MEM_WIDTH), tile(N_BRANCH * D_MODEL)]
                 + [_const_spec(c.shape) for c in consts],
        out_specs=tile(D_MODEL),
        out_shape=jax.ShapeDtypeStruct((batch, seq, D_MODEL), F32),
        scratch_shapes=[pltpu.VMEM((D_FF // fc, SUBLANES, fc), F32), pltpu.VMEM((tm, D_FF), BF16)],
        compiler_params=_params(("parallel", "arbitrary")),
        name="merge_ffn",
    )(x3, ya, yb, yc, zgate3, *consts)


def kernel(x, mem, norm_mix_g, norm_mem_g, w_in, hgrn_lb_logits, hgrn_norm_g, fox_f_bias, fox_q_norm_g,
           fox_k_norm_g, mem_kv_w, mem_q_norm_g, mem_k_norm_g, w_br_hgrn, w_br_fox, w_br_mem, w_out,
           norm_ffn_g, ffn_w_up, ffn_conv_w, ffn_conv_b, ffn_w_down):
    batch, seq, _ = x.shape
    n_mem = mem.shape[1]
    depth = w_in.shape[0]
    assert depth == 1 and seq % 512 == 0

    x2 = x.reshape(batch * seq, D_MODEL)
    for l in range(depth):
        w = w_in[l]
        c_hg, c_fox = 4 * HG_WIDTH, 3 * FOX_WIDTH
        o_ff = c_hg + c_fox
        o_mq = o_ff + FOX_HEADS
        o_gate = o_mq + MEM_WIDTH
        whg = w[:, :c_hg].astype(BF16)
        wfox = w[:, c_hg:o_ff].astype(BF16)
        wff = jnp.pad(w[:, o_ff:o_mq], ((0, 0), (0, LANES - FOX_HEADS))).astype(BF16)
        wmq = w[:, o_mq:o_gate].astype(BF16)
        wgate = w[:, o_gate:].astype(BF16)
        row = lambda v: v.reshape(1, -1).astype(F32)

        zhg, zfox, zmq, zgate, zff = _in_proj(x2, row(norm_mix_g[l]), whg, wfox, wmq, wgate, wff, tm=512)

        ya = _hgrn(zhg, hgrn_lb_logits.astype(F32), row(hgrn_norm_g[l]), batch, seq)

        zfox3 = zfox.reshape(batch, seq, c_fox)
        fb = jnp.pad(row(fox_f_bias[l]), ((0, 0), (0, LANES - FOX_HEADS)))
        head_id = jnp.arange(MXU_WIDTH) // FOX_DH
        bd = (head_id[:, None] == head_id[None, :]).astype(BF16) * (1.0 / FOX_DH)
        qa, ka, vt = _fox_prep(zfox3, zff.reshape(batch, seq, LANES), fb,
                               jnp.tile(row(fox_q_norm_g[l]), (1, FOX_HEADS)),
                               jnp.tile(row(fox_k_norm_g[l]), (1, FOX_HEADS)), bd, batch, seq)
        qk_bound = (LOG2E * FOX_DH ** 0.5 * jnp.max(jnp.abs(fox_q_norm_g[l])) * jnp.max(jnp.abs(fox_k_norm_g[l])))
        yb = _fox_attn(qk_bound.reshape(1).astype(F32), qa, ka, vt, batch, seq, tq=512)

        mk, mv = _mem_kv(mem.reshape(batch * n_mem, D_MODEL), row(norm_mem_g[l]), mem_kv_w[l].astype(BF16),
                         row(mem_k_norm_g[l]), tm=512)
        yc = _mem_attn(zmq.reshape(batch, seq, MEM_WIDTH), mk.reshape(batch, n_mem, MEM_WIDTH),
                       mv.reshape(batch, n_mem, MEM_WIDTH), row(mem_q_norm_g[l]), batch, seq, tq=512)

        wup = ffn_w_up[l]
        out = _merge_ffn(x2.reshape(batch, seq, D_MODEL), ya, yb, yc, zgate.reshape(batch, seq, N_BRANCH * D_MODEL),
                         w_br_hgrn[l].astype(BF16), w_br_fox[l].astype(BF16), w_br_mem[l].astype(BF16),
                         w_out[l].astype(BF16), row(norm_ffn_g[l]), wup[:, :D_FF].astype(BF16),
                         wup[:, D_FF:].astype(BF16), ffn_conv_w[l].astype(F32), row(ffn_conv_b[l]),
                         ffn_w_down[l].astype(BF16), tm=512, fc=256)
        x2 = out.reshape(batch * seq, D_MODEL)
    return x2.reshape(batch, seq, D_MODEL)
```

```python
import functools

import numpy as np
import jax
import jax.numpy as jnp
from jax import lax
from jax.experimental import pallas as pl
from jax.experimental.pallas import tpu as pltpu

F32 = jnp.float32
BF16 = jnp.bfloat16

D_MODEL = 1024
EPS = 1e-6
CHUNK = 64
SUB = 16
HG_HEADS = 4
HG_D = 128
HG_WIDTH = HG_HEADS * HG_D
FOX_HEADS = 8
FOX_DH = 64
FOX_WIDTH = FOX_HEADS * FOX_DH
MEM_HEADS = 4
MEM_DH = 128
MEM_WIDTH = MEM_HEADS * MEM_DH
N_BRANCH = 3
D_FF = 2816
CONV_W = 3

LANES = 128
SUBLANES = 8
MXU_WIDTH = 256
VMEM_LIMIT = 56 * 1024 * 1024

NEG_BIG = -1e30


def _const_spec(shape):
    nd = len(shape)
    return pl.BlockSpec(shape, lambda *_: (0,) * nd, pipeline_mode=pl.Buffered(1))


def _params(sem):
    return pltpu.CompilerParams(dimension_semantics=sem, vmem_limit_bytes=VMEM_LIMIT)


def _rms_rows(x, g):
    return x * lax.rsqrt(jnp.mean(x * x, axis=-1, keepdims=True) + EPS) * g


def _sigmoid(x):
    return 1.0 / (1.0 + jnp.exp(-x))


def _silu(x):
    return x * _sigmoid(x)


def _inproj_kernel(x_ref, g_ref, whg_ref, wfox_ref, wmq_ref, wgate_ref, wff_ref,
                   zhg_ref, zfox_ref, zmq_ref, zgate_ref, zff_ref, *, tn):
    h = _rms_rows(x_ref[...], g_ref[...]).astype(BF16)
    for w_ref, o_ref in ((whg_ref, zhg_ref), (wfox_ref, zfox_ref), (wmq_ref, zmq_ref),
                         (wgate_ref, zgate_ref), (wff_ref, zff_ref)):
        n = w_ref.shape[1]
        step = min(tn, n)
        for c in range(0, n, step):
            o_ref[:, c:c + step] = jnp.dot(h, w_ref[:, c:c + step],
                                           preferred_element_type=F32).astype(o_ref.dtype)


def _in_proj(x2, g, whg, wfox, wmq, wgate, wff, tm):
    n_rows = x2.shape[0]
    ws = (whg, wfox, wmq, wgate, wff)
    out_dtypes = (BF16, BF16, BF16, BF16, F32)
    row = lambda i: (i, 0)
    return pl.pallas_call(
        functools.partial(_inproj_kernel, tn=512),
        grid=(n_rows // tm,),
        in_specs=[pl.BlockSpec((tm, D_MODEL), row), _const_spec(g.shape)]
                 + [_const_spec(w.shape) for w in ws],
        out_specs=[pl.BlockSpec((tm, w.shape[1]), row) for w in ws],
        out_shape=[jax.ShapeDtypeStruct((n_rows, w.shape[1]), dt) for w, dt in zip(ws, out_dtypes)],
        compiler_params=_params(("parallel",)),
        name="in_proj",
    )(x2, g, *ws)


HG_LEVELS = (32, 16, 8, 4, 2, 1)
LOG2E = 1.4426950408889634


def _hgrn_pivot(g, h, row):
    n, w = g.shape
    if h >= SUBLANES:
        parts = [jnp.broadcast_to(g[p:p + 1], (2 * h, w)) for p in range(h - 1, n, 2 * h)]
        return parts[0] if len(parts) == 1 else jnp.concatenate(parts, axis=0)
    if h == 1:
        return jnp.where((row & 1) != 0, pltpu.roll(g, 1, axis=0), g)
    parts = []
    row8 = lax.broadcasted_iota(jnp.int32, (SUBLANES, w), 0)
    for lo in range(0, n, SUBLANES):
        cands = [jnp.broadcast_to(g[p:p + 1], (SUBLANES, w)) for p in range(lo + h - 1, lo + SUBLANES, 2 * h)]
        piv = cands[-1]
        for j in range(len(cands) - 2, -1, -1):
            piv = jnp.where(row8 < (j + 1) * 2 * h, cands[j], piv)
        parts.append(piv)
    return jnp.concatenate(parts, axis=0)


def _hgrn_kernel(q_ref, f_ref, i_ref, go_ref, lbl_ref, ng_ref, o_ref, st_ref, *, seq):
    lbl = lbl_ref[...]
    e = jnp.exp(lbl - jnp.max(lbl, axis=0, keepdims=True))
    lb = e[0:1] / jnp.sum(e, axis=0, keepdims=True)
    ng = ng_ref[...]
    st_ref[...] = jnp.zeros_like(st_ref)

    row = lax.broadcasted_iota(jnp.int32, (CHUNK, HG_WIDTH), 0)
    row_a = lax.broadcasted_iota(jnp.int32, (CHUNK, CHUNK), 0)
    col_a = lax.broadcasted_iota(jnp.int32, (CHUNK, CHUNK), 1)
    nt = (((1,), (1,)), ((), ()))
    heads = [slice(hd * HG_D, (hd + 1) * HG_D) for hd in range(HG_HEADS)]
    n_chunks = seq // CHUNK

    def chunk_rows(c):
        return pl.ds(pl.multiple_of(c * CHUNK, CHUNK), CHUNK)

    def local_part(c):
        rows = chunk_rows(c)
        f = lb + (1.0 - lb) * _sigmoid(f_ref[rows, :].astype(F32))
        k_all = 1.0 - f
        g_all = jnp.log(f) * LOG2E
        sh = 1
        while sh < CHUNK:
            g_all = g_all + jnp.where(row >= sh, pltpu.roll(g_all, sh, axis=0), 0.0)
            sh *= 2
        q_all = _silu(q_ref[rows, :].astype(F32))
        go_all = _silu(go_ref[rows, :].astype(F32))
        a = [None] * HG_HEADS
        for h in HG_LEVELS:
            d = jnp.exp2(-jnp.abs(g_all - _hgrn_pivot(g_all, h, row)))
            t = (jnp.where((row & h) != 0, q_all, k_all) * d).astype(BF16)
            keep = ((row_a // (2 * h)) == (col_a // (2 * h))) & ((row_a & h) != 0) & ((col_a & h) == 0)
            for hd, sl in enumerate(heads):
                prod = lax.dot_general(t[:, sl], t[:, sl], nt, preferred_element_type=F32)
                a[hd] = jnp.where(keep, prod, 0.0 if a[hd] is None else a[hd])
        qk = q_all * k_all
        for hd, sl in enumerate(heads):
            a[hd] = jnp.where(row_a == col_a, jnp.sum(qk[:, sl], axis=-1, keepdims=True), a[hd]).astype(BF16)

        g_last = g_all[CHUNK - 1:CHUNK]
        q_in = (q_all * jnp.exp2(g_all)).astype(BF16)
        k_up = (k_all * jnp.exp2(g_last - g_all)).astype(BF16)
        return a, q_in, k_up, go_all, jnp.exp2(g_last)

    def state_part(c, vals):
        rows = chunk_rows(c)
        i_all = i_ref[rows, :]
        a, q_in, k_up, go_all, dec = vals
        for hd, sl in enumerate(heads):
            st = st_ref[hd]
            o = jnp.dot(jnp.concatenate([q_in[:, sl], a[hd]], axis=1),
                        jnp.concatenate([st.T.astype(BF16), i_all[:, sl]], axis=0),
                        preferred_element_type=F32)
            i_t = i_all[:, sl].astype(F32).T.astype(BF16)
            st_ref[hd] = st * dec[:, sl] + jnp.dot(i_t, k_up[:, sl], preferred_element_type=F32)
            o_ref[rows, sl] = (_rms_rows(o, ng) * go_all[:, sl]).astype(o_ref.dtype)

    def step(k, carry):
        c = 2 * k
        first = local_part(c)
        second = local_part(c + 1)
        state_part(c, first)
        state_part(c + 1, second)
        return carry

    assert n_chunks % 2 == 0
    lax.fori_loop(0, n_chunks // 2, step, 0)


def _hgrn(zhg, lb_logits, norm_g, batch, seq):
    zhg3 = zhg.reshape(batch, seq, 4 * HG_WIDTH)
    blk = lambda off: pl.BlockSpec((None, seq, HG_WIDTH), lambda b: (b, 0, off))
    return pl.pallas_call(
        functools.partial(_hgrn_kernel, seq=seq),
        grid=(batch,),
        in_specs=[blk(0), blk(1), blk(2), blk(3), _const_spec(lb_logits.shape), _const_spec(norm_g.shape)],
        out_specs=pl.BlockSpec((None, seq, HG_WIDTH), lambda b: (b, 0, 0)),
        out_shape=jax.ShapeDtypeStruct((batch, seq, HG_WIDTH), BF16),
        scratch_shapes=[pltpu.VMEM((HG_HEADS, HG_D, HG_D), F32)],
        compiler_params=_params(("parallel",)),
        name="hgrn",
    )(zhg3, zhg3, zhg3, zhg3, lb_logits, norm_g)


FOX_AUG = FOX_DH
FOX_NAUG = 6


def _split3(x):
    hi = x.astype(BF16).astype(F32)
    r = x - hi
    lo = r.astype(BF16).astype(F32)
    return hi, lo, r - lo


def _foxprep_kernel(q_ref, k_ref, v_ref, ff_ref, fb_ref, qg_ref, kg_ref, bd_ref, place_ref,
                    qa_ref, ka_ref, vt_ref, *, seq):
    for p in range(FOX_HEADS // 2):
        vt_ref[p] = v_ref[:, p * LANES:(p + 1) * LANES].astype(F32).T.astype(BF16)
    x = ff_ref[...] + fb_ref[...]
    fc = -(jnp.maximum(-x, 0.0) + jnp.log1p(jnp.exp(-jnp.abs(x))))
    pos = lax.broadcasted_iota(jnp.int32, (seq, LANES), 0)
    sh = 1
    while sh < seq:
        fc = fc + jnp.where(pos >= sh, pltpu.roll(fc, sh, axis=0), 0.0)
        sh *= 2
    fc = fc * LOG2E

    def normed(ref, g_ref, scale):
        v = ref[...].astype(F32)
        sq = (v * v).astype(BF16)
        w = bd_ref.shape[0]
        ms = jnp.concatenate([jnp.dot(sq[:, c:c + w], bd_ref[...], preferred_element_type=F32)
                              for c in range(0, FOX_WIDTH, w)], axis=1)
        return v * lax.rsqrt(ms + EPS) * (g_ref[...] * scale)

    qn = normed(q_ref, qg_ref, FOX_DH ** -0.5 * LOG2E)
    kn = normed(k_ref, kg_ref, 1.0)
    lane = lax.broadcasted_iota(jnp.int32, (seq, LANES), 1)
    hi, lo, lo2 = _split3(fc)
    hi = jnp.where(lane == FOX_HEADS, 1.0, hi)
    aug = jnp.dot(jnp.concatenate([hi, lo, lo2], axis=1).astype(BF16), place_ref[...],
                  preferred_element_type=F32)
    for h in range(FOX_HEADS):
        aug_q = pltpu.roll(aug, (FOX_AUG - FOX_NAUG * h) % LANES, axis=1)
        aug_k = pltpu.roll(aug, (FOX_AUG - FOX_DH - FOX_NAUG * h) % LANES, axis=1) if h else aug
        pair = slice(LANES * (h // 2), LANES * (h // 2) + LANES)
        pq, pk = qn[:, pair], kn[:, pair]
        if h % 2:
            pq = pltpu.roll(pq, FOX_DH, axis=1)
            pk = pltpu.roll(pk, FOX_DH, axis=1)
        qa_ref[h] = jnp.where(lane < FOX_DH, pq, aug_q).astype(BF16)
        ka_ref[h] = jnp.where(lane < FOX_DH, pk, jnp.where(lane < FOX_AUG + FOX_NAUG, aug_k, 0.0)).astype(BF16)


def _fox_place_matrix():
    place = np.zeros((3 * LANES, LANES), np.float32)
    for h in range(FOX_HEADS):
        for part in range(3):
            place[part * LANES + h, FOX_NAUG * h + part] = 1.0
            place[part * LANES + h, FOX_DH + FOX_NAUG * h + 3 + part] = -1.0
            place[FOX_HEADS, FOX_NAUG * h + 3 + part] = 1.0
            place[FOX_HEADS, FOX_DH + FOX_NAUG * h + part] = 1.0
    return jnp.asarray(place, BF16)


def _fox_prep(zfox3, zff3, fb, qg, kg, bd, batch, seq):
    place = _fox_place_matrix()
    head_out = pl.BlockSpec((None, FOX_HEADS, seq, LANES), lambda b: (b, 0, 0, 0))
    return pl.pallas_call(
        functools.partial(_foxprep_kernel, seq=seq),
        grid=(batch,),
        in_specs=[pl.BlockSpec((None, seq, FOX_WIDTH), lambda b: (b, 0, 0)),
                  pl.BlockSpec((None, seq, FOX_WIDTH), lambda b: (b, 0, 1)),
                  pl.BlockSpec((None, seq, FOX_WIDTH), lambda b: (b, 0, 2)),
                  pl.BlockSpec((None, seq, LANES), lambda b: (b, 0, 0)),
                  _const_spec(fb.shape), _const_spec(qg.shape), _const_spec(kg.shape), _const_spec(bd.shape),
                  _const_spec(place.shape)],
        out_specs=[head_out, head_out,
                   pl.BlockSpec((None, FOX_HEADS // 2, LANES, seq), lambda b: (b, 0, 0, 0))],
        out_shape=[jax.ShapeDtypeStruct((batch, FOX_HEADS, seq, LANES), BF16)] * 2
                  + [jax.ShapeDtypeStruct((batch, FOX_HEADS // 2, LANES, seq), BF16)],
        compiler_params=_params(("parallel",)),
        name="fox_prep",
    )(zfox3, zfox3, zfox3, zff3, fb, qg, kg, bd, place)


FOX_FAST_LOGIT_BOUND = 100.0


def _fox_fast(qa_ref, ka_ref, vt_ref, o_ref, tq, n_blk):
    key = lax.broadcasted_iota(jnp.int32, (tq, tq), 0)
    qry = lax.broadcasted_iota(jnp.int32, (tq, tq), 1)
    nt = (((1,), (1,)), ((), ()))
    for i in range(n_blk):
        l = [None, None]
        acc = [None, None]
        for j in range(i + 1):
            ss = [lax.dot_general(ka_ref[hh, j * tq:(j + 1) * tq, :], qa_ref[hh, i * tq:(i + 1) * tq, :],
                                  nt, preferred_element_type=F32) for hh in range(2)]
            for hh in range(2):
                s = jnp.where(key <= qry, ss[hh], NEG_BIG) if j == i else ss[hh]
                p = jnp.exp2(s)
                vt = vt_ref[hh * FOX_DH:(hh + 1) * FOX_DH, j * tq:(j + 1) * tq]
                ps = jnp.sum(p, axis=0, keepdims=True)
                pv = jnp.dot(vt, p.astype(BF16), preferred_element_type=F32)
                l[hh] = ps if j == 0 else l[hh] + ps
                acc[hh] = pv if j == 0 else acc[hh] + pv
        out_t = jnp.concatenate([acc[hh] / l[hh] for hh in range(2)], axis=0)
        o_ref[i * tq:(i + 1) * tq, :] = out_t.T.astype(o_ref.dtype)


def _foxattn_kernel(bound_ref, qa_ref, ka_ref, vt_ref, o_ref, s_ref, m_ref, l_ref, acc_ref, *, tq, n_blk):
    fast = bound_ref[0] <= FOX_FAST_LOGIT_BOUND

    @pl.when(fast)
    def _():
        _fox_fast(qa_ref, ka_ref, vt_ref, o_ref, tq, n_blk)

    @pl.when(jnp.logical_not(fast))
    def _():
        _fox_online(qa_ref, ka_ref, vt_ref, o_ref, s_ref, m_ref, l_ref, acc_ref, tq, n_blk)


def _fox_online(qa_ref, ka_ref, vt_ref, o_ref, s_ref, m_ref, l_ref, acc_ref, tq, n_blk):
    key = lax.broadcasted_iota(jnp.int32, (tq, tq), 0)
    qry = lax.broadcasted_iota(jnp.int32, (tq, tq), 1)
    nt = (((1,), (1,)), ((), ()))
    steps = [(i, j) for i in range(n_blk) for j in range(i + 1)]

    def scores(n, hh):
        i, j = steps[n]
        s_ref[n % 2, hh] = lax.dot_general(ka_ref[hh, j * tq:(j + 1) * tq, :], qa_ref[hh, i * tq:(i + 1) * tq, :],
                                           nt, preferred_element_type=F32)

    def reduce(n, hh):
        i, j = steps[n]
        s = s_ref[n % 2, hh]
        if j == i:
            s = jnp.where(key <= qry, s, NEG_BIG)
        m_new = jnp.max(s, axis=0, keepdims=True)
        vt = vt_ref[hh * FOX_DH:(hh + 1) * FOX_DH, j * tq:(j + 1) * tq]
        if j == 0:
            p = jnp.exp2(s - m_new)
            m_ref[hh] = m_new
            l_ref[hh] = jnp.sum(p, axis=0, keepdims=True)
            acc_ref[hh] = jnp.dot(vt, p.astype(BF16), preferred_element_type=F32)
        else:
            m = m_ref[hh]
            m_new = jnp.maximum(m, m_new)
            alpha = jnp.exp2(m - m_new)
            p = jnp.exp2(s - m_new)
            m_ref[hh] = m_new
            l_ref[hh] = alpha * l_ref[hh] + jnp.sum(p, axis=0, keepdims=True)
            acc_ref[hh] = alpha * acc_ref[hh] + jnp.dot(vt, p.astype(BF16), preferred_element_type=F32)

    for hh in range(2):
        scores(0, hh)
    for n, (i, j) in enumerate(steps):
        for hh in range(2):
            if n + 1 < len(steps):
                scores(n + 1, hh)
            reduce(n, hh)
        if j == i:
            out_t = jnp.concatenate([acc_ref[hh] / l_ref[hh] for hh in range(2)], axis=0)
            o_ref[i * tq:(i + 1) * tq, :] = out_t.T.astype(o_ref.dtype)


def _fox_attn(qk_bound, qa, ka, vt, batch, seq, tq):
    pair = lambda b, p: (b, p, 0, 0)
    return pl.pallas_call(
        functools.partial(_foxattn_kernel, tq=tq, n_blk=seq // tq),
        grid=(batch, FOX_HEADS // 2),
        in_specs=[pl.BlockSpec(memory_space=pltpu.SMEM),
                  pl.BlockSpec((None, 2, seq, LANES), pair), pl.BlockSpec((None, 2, seq, LANES), pair),
                  pl.BlockSpec((None, None, LANES, seq), pair)],
        out_specs=pl.BlockSpec((None, seq, LANES), lambda b, p: (b, 0, p)),
        out_shape=jax.ShapeDtypeStruct((batch, seq, FOX_WIDTH), BF16),
        scratch_shapes=[pltpu.VMEM((2, 2, tq, tq), F32), pltpu.VMEM((2, 1, tq), F32), pltpu.VMEM((2, 1, tq), F32),
                        pltpu.VMEM((2, FOX_DH, tq), F32)],
        compiler_params=_params(("parallel", "parallel")),
        name="fox_attn",
    )(qk_bound, qa, ka, vt)


def _memkv_kernel(m_ref, g_ref, w_ref, kg_ref, k_ref, v_ref):
    h = _rms_rows(m_ref[...], g_ref[...]).astype(BF16)
    kv = jnp.dot(h, w_ref[...], preferred_element_type=F32)
    for hd in range(MEM_HEADS):
        sl = slice(hd * MEM_DH, (hd + 1) * MEM_DH)
        k_ref[:, sl] = _rms_rows(kv[:, sl], kg_ref[...]).astype(k_ref.dtype)
    v_ref[...] = kv[:, MEM_WIDTH:].astype(v_ref.dtype)


def _mem_kv(mem2, g, w, kg, tm):
    n_rows = mem2.shape[0]
    row = lambda i: (i, 0)
    return pl.pallas_call(
        _memkv_kernel,
        grid=(n_rows // tm,),
        in_specs=[pl.BlockSpec((tm, D_MODEL), row), _const_spec(g.shape), _const_spec(w.shape),
                  _const_spec(kg.shape)],
        out_specs=[pl.BlockSpec((tm, MEM_WIDTH), row)] * 2,
        out_shape=[jax.ShapeDtypeStruct((n_rows, MEM_WIDTH), BF16)] * 2,
        compiler_params=_params(("parallel",)),
        name="mem_kv",
    )(mem2, g, w, kg)


def _memattn_kernel(q_ref, k_ref, v_ref, qg_ref, o_ref):
    for hd in range(MEM_HEADS):
        sl = slice(hd * MEM_DH, (hd + 1) * MEM_DH)
        q = _rms_rows(q_ref[:, sl].astype(F32), qg_ref[...] * (MEM_DH ** -0.5)).astype(BF16)
        s = lax.dot_general(q, k_ref[:, sl], (((1,), (1,)), ((), ())), preferred_element_type=F32)
        p = jnp.exp(s - jnp.max(s, axis=-1, keepdims=True))
        l = jnp.sum(p, axis=-1, keepdims=True)
        o = jnp.dot(p.astype(BF16), v_ref[:, sl], preferred_element_type=F32)
        o_ref[:, sl] = (o / l).astype(o_ref.dtype)


def _mem_attn(zmq3, mk3, mv3, qg, batch, seq, tq):
    n_mem = mk3.shape[1]
    kv_spec = pl.BlockSpec((None, n_mem, MEM_WIDTH), lambda b, i: (b, 0, 0))
    return pl.pallas_call(
        _memattn_kernel,
        grid=(batch, seq // tq),
        in_specs=[pl.BlockSpec((None, tq, MEM_WIDTH), lambda b, i: (b, i, 0)), kv_spec, kv_spec,
                  _const_spec(qg.shape)],
        out_specs=pl.BlockSpec((None, tq, MEM_WIDTH), lambda b, i: (b, i, 0)),
        out_shape=jax.ShapeDtypeStruct((batch, seq, MEM_WIDTH), BF16),
        compiler_params=_params(("parallel", "parallel")),
        name="mem_attn",
    )(zmq3, mk3, mv3, qg)


def _mergeffn_kernel(x_ref, ya_ref, yb_ref, yc_ref, gt_ref, wba_ref, wbb_ref, wbc_ref, wo_ref, ng_ref,
                     wa_ref, wv_ref, cw_ref, cb_ref, wd_ref, o_ref, carry_ref, y_ref, *, tm, fc):
    @pl.when(pl.program_id(1) == 0)
    def _():
        carry_ref[...] = jnp.zeros_like(carry_ref)

    merged = None
    for br, (yb_r, w_ref) in enumerate(((ya_ref, wba_ref), (yb_ref, wbb_ref), (yc_ref, wbc_ref))):
        gate = _sigmoid(gt_ref[:, br * D_MODEL:(br + 1) * D_MODEL].astype(F32))
        t = gate * jnp.dot(yb_r[...], w_ref[...], preferred_element_type=F32)
        merged = t if merged is None else merged + t
    x1 = x_ref[...] + jnp.dot(merged.astype(BF16), wo_ref[...], preferred_element_type=F32)
    o_ref[...] = x1

    h = _rms_rows(x1, ng_ref[...]).astype(BF16)
    row = lax.broadcasted_iota(jnp.int32, (tm, fc), 0)
    for c in range(D_FF // fc):
        sl = slice(c * fc, (c + 1) * fc)
        a = jnp.dot(h, wa_ref[:, sl], preferred_element_type=F32)
        v = jnp.dot(h, wv_ref[:, sl], preferred_element_type=F32)
        prev = carry_ref[c]
        p1 = prev[SUBLANES - 1:SUBLANES]
        p2 = prev[SUBLANES - 2:SUBLANES - 1]
        a1 = jnp.where(row == 0, p1, pltpu.roll(a, 1, axis=0))
        a2 = jnp.where(row == 0, p2, jnp.where(row == 1, p1, pltpu.roll(a, 2, axis=0)))
        carry_ref[c] = a[tm - SUBLANES:tm]
        z = a2 * cw_ref[0:1, sl] + a1 * cw_ref[1:2, sl] + a * cw_ref[2:3, sl] + cb_ref[:, sl]
        gelu = 0.5 * z * (1.0 + lax.erf(z * (2.0 ** -0.5)))
        y_ref[:, sl] = (gelu * v).astype(y_ref.dtype)
    o_ref[...] += jnp.dot(y_ref[...], wd_ref[...], preferred_element_type=F32)


def _merge_ffn(x3, ya, yb, yc, zgate3, wba, wbb, wbc, wo, ng, wa, wv, cw, cb, wd, tm, fc):
    batch, seq, _ = x3.shape
    tile = lambda w: pl.BlockSpec((None, tm, w), lambda b, t: (b, t, 0))
    consts = (wba, wbb, wbc, wo, ng, wa, wv, cw, cb, wd)
    return pl.pallas_call(
        functools.partial(_mergeffn_kernel, tm=tm, fc=fc),
        grid=(batch, seq // tm),
        in_specs=[tile(D_MODEL), tile(HG_WIDTH), tile(FOX_WIDTH), tile(MEM_WIDTH), tile(N_BRANCH * D_MODEL)]
                 + [_const_spec(c.shape) for c in consts],
        out_specs=tile(D_MODEL),
        out_shape=jax.ShapeDtypeStruct((batch, seq, D_MODEL), F32),
        scratch_shapes=[pltpu.VMEM((D_FF // fc, SUBLANES, fc), F32), pltpu.VMEM((tm, D_FF), BF16)],
        compiler_params=_params(("parallel", "arbitrary")),
        name="merge_ffn",
    )(x3, ya, yb, yc, zgate3, *consts)


def kernel(x, mem, norm_mix_g, norm_mem_g, w_in, hgrn_lb_logits, hgrn_norm_g, fox_f_bias, fox_q_norm_g,
           fox_k_norm_g, mem_kv_w, mem_q_norm_g, mem_k_norm_g, w_br_hgrn, w_br_fox, w_br_mem, w_out,
           norm_ffn_g, ffn_w_up, ffn_conv_w, ffn_conv_b, ffn_w_down):
    batch, seq, _ = x.shape
    n_mem = mem.shape[1]
    depth = w_in.shape[0]
    assert depth == 1 and seq % 512 == 0

    x2 = x.reshape(batch * seq, D_MODEL)
    for l in range(depth):
        w = w_in[l]
        c_hg, c_fox = 4 * HG_WIDTH, 3 * FOX_WIDTH
        o_ff = c_hg + c_fox
        o_mq = o_ff + FOX_HEADS
        o_gate = o_mq + MEM_WIDTH
        whg = w[:, :c_hg].astype(BF16)
        wfox = w[:, c_hg:o_ff].astype(BF16)
        wff = jnp.pad(w[:, o_ff:o_mq], ((0, 0), (0, LANES - FOX_HEADS))).astype(BF16)
        wmq = w[:, o_mq:o_gate].astype(BF16)
        wgate = w[:, o_gate:].astype(BF16)
        row = lambda v: v.reshape(1, -1).astype(F32)

        zhg, zfox, zmq, zgate, zff = _in_proj(x2, row(norm_mix_g[l]), whg, wfox, wmq, wgate, wff, tm=512)

        ya = _hgrn(zhg, hgrn_lb_logits.astype(F32), row(hgrn_norm_g[l]), batch, seq)

        zfox3 = zfox.reshape(batch, seq, c_fox)
        fb = jnp.pad(row(fox_f_bias[l]), ((0, 0), (0, LANES - FOX_HEADS)))
        head_id = jnp.arange(MXU_WIDTH) // FOX_DH
        bd = (head_id[:, None] == head_id[None, :]).astype(BF16) * (1.0 / FOX_DH)
        qa, ka, vt = _fox_prep(zfox3, zff.reshape(batch, seq, LANES), fb,
                               jnp.tile(row(fox_q_norm_g[l]), (1, FOX_HEADS)),
                               jnp.tile(row(fox_k_norm_g[l]), (1, FOX_HEADS)), bd, batch, seq)
        qk_bound = (LOG2E * FOX_DH ** 0.5 * jnp.max(jnp.abs(fox_q_norm_g[l])) * jnp.max(jnp.abs(fox_k_norm_g[l])))
        yb = _fox_attn(qk_bound.reshape(1).astype(F32), qa, ka, vt, batch, seq, tq=512)

        mk, mv = _mem_kv(mem.reshape(batch * n_mem, D_MODEL), row(norm_mem_g[l]), mem_kv_w[l].astype(BF16),
                         row(mem_k_norm_g[l]), tm=512)
        yc = _mem_attn(zmq.reshape(batch, seq, MEM_WIDTH), mk.reshape(batch, n_mem, MEM_WIDTH),
                       mv.reshape(batch, n_mem, MEM_WIDTH), row(mem_q_norm_g[l]), batch, seq, tq=512)

        wup = ffn_w_up[l]
        out = _merge_ffn(x2.reshape(batch, seq, D_MODEL), ya, yb, yc, zgate.reshape(batch, seq, N_BRANCH * D_MODEL),
                         w_br_hgrn[l].astype(BF16), w_br_fox[l].astype(BF16), w_br_mem[l].astype(BF16),
                         w_out[l].astype(BF16), row(norm_ffn_g[l]), wup[:, :D_FF].astype(BF16),
                         wup[:, D_FF:].astype(BF16), ffn_conv_w[l].astype(F32), row(ffn_conv_b[l]),
                         ffn_w_down[l].astype(BF16), tm=512, fc=256)
        x2 = out.reshape(batch * seq, D_MODEL)
    return x2.reshape(batch, seq, D_MODEL)
```

```python
import functools

import numpy as np
import jax
import jax.numpy as jnp
from jax import lax
from jax.experimental import pallas as pl
from jax.experimental.pallas import tpu as pltpu

F32 = jnp.float32
BF16 = jnp.bfloat16

D_MODEL = 1024
EPS = 1e-6
CHUNK = 64
SUB = 16
HG_HEADS = 4
HG_D = 128
HG_WIDTH = HG_HEADS * HG_D
FOX_HEADS = 8
FOX_DH = 64
FOX_WIDTH = FOX_HEADS * FOX_DH
MEM_HEADS = 4
MEM_DH = 128
MEM_WIDTH = MEM_HEADS * MEM_DH
N_BRANCH = 3
D_FF = 2816
CONV_W = 3

LANES = 128
SUBLANES = 8
MXU_WIDTH = 256
VMEM_LIMIT = 56 * 1024 * 1024

NEG_BIG = -1e30


def _const_spec(shape):
    nd = len(shape)
    return pl.BlockSpec(shape, lambda *_: (0,) * nd, pipeline_mode=pl.Buffered(1))


def _params(sem):
    return pltpu.CompilerParams(dimension_semantics=sem, vmem_limit_bytes=VMEM_LIMIT)


def _rms_rows(x, g):
    return x * lax.rsqrt(jnp.mean(x * x, axis=-1, keepdims=True) + EPS) * g


def _sigmoid(x):
    return 1.0 / (1.0 + jnp.exp(-x))


def _silu(x):
    return x * _sigmoid(x)


def _inproj_kernel(x_ref, g_ref, whg_ref, wfox_ref, wmq_ref, wgate_ref, wfft_ref,
                   zhg_ref, zfox_ref, zmq_ref, zgate_ref, zfft_ref, *, tn):
    h = _rms_rows(x_ref[...], g_ref[...]).astype(BF16)
    for w_ref, o_ref in ((whg_ref, zhg_ref), (wfox_ref, zfox_ref), (wmq_ref, zmq_ref), (wgate_ref, zgate_ref)):
        n = w_ref.shape[1]
        step = min(tn, n)
        for c in range(0, n, step):
            o_ref[:, c:c + step] = jnp.dot(h, w_ref[:, c:c + step],
                                           preferred_element_type=F32).astype(o_ref.dtype)
    zfft_ref[...] = lax.dot_general(wfft_ref[...], h, (((1,), (1,)), ((), ())), preferred_element_type=F32)


def _in_proj(x2, g, whg, wfox, wmq, wgate, wfft, tm):
    n_rows = x2.shape[0]
    ws = (whg, wfox, wmq, wgate)
    row = lambda i: (i, 0)
    return pl.pallas_call(
        functools.partial(_inproj_kernel, tn=512),
        grid=(n_rows // tm,),
        in_specs=[pl.BlockSpec((tm, D_MODEL), row), _const_spec(g.shape)]
                 + [_const_spec(w.shape) for w in ws] + [_const_spec(wfft.shape)],
        out_specs=[pl.BlockSpec((tm, w.shape[1]), row) for w in ws]
                  + [pl.BlockSpec((wfft.shape[0], tm), lambda i: (0, i))],
        out_shape=[jax.ShapeDtypeStruct((n_rows, w.shape[1]), BF16) for w in ws]
                  + [jax.ShapeDtypeStruct((wfft.shape[0], n_rows), F32)],
        compiler_params=_params(("parallel",)),
        name="in_proj",
    )(x2, g, *ws, wfft)


HG_LEVELS = (32, 16, 8, 4, 2, 1)
LOG2E = 1.4426950408889634


def _hgrn_pivot(g, h, row):
    n, w = g.shape
    if h >= SUBLANES:
        parts = [jnp.broadcast_to(g[p:p + 1], (2 * h, w)) for p in range(h - 1, n, 2 * h)]
        return parts[0] if len(parts) == 1 else jnp.concatenate(parts, axis=0)
    if h == 1:
        return jnp.where((row & 1) != 0, pltpu.roll(g, 1, axis=0), g)
    parts = []
    row8 = lax.broadcasted_iota(jnp.int32, (SUBLANES, w), 0)
    for lo in range(0, n, SUBLANES):
        cands = [jnp.broadcast_to(g[p:p + 1], (SUBLANES, w)) for p in range(lo + h - 1, lo + SUBLANES, 2 * h)]
        piv = cands[-1]
        for j in range(len(cands) - 2, -1, -1):
            piv = jnp.where(row8 < (j + 1) * 2 * h, cands[j], piv)
        parts.append(piv)
    return jnp.concatenate(parts, axis=0)


def _hgrn_kernel(q_ref, f_ref, i_ref, go_ref, lbl_ref, ng_ref, o_ref, st_ref, *, seq):
    lbl = lbl_ref[...]
    e = jnp.exp(lbl - jnp.max(lbl, axis=0, keepdims=True))
    lb = e[0:1] / jnp.sum(e, axis=0, keepdims=True)
    ng = ng_ref[...]
    st_ref[...] = jnp.zeros_like(st_ref)

    row = lax.broadcasted_iota(jnp.int32, (CHUNK, HG_WIDTH), 0)
    row_a = lax.broadcasted_iota(jnp.int32, (CHUNK, CHUNK), 0)
    col_a = lax.broadcasted_iota(jnp.int32, (CHUNK, CHUNK), 1)
    nt = (((1,), (1,)), ((), ()))
    heads = [slice(hd * HG_D, (hd + 1) * HG_D) for hd in range(HG_HEADS)]
    n_chunks = seq // CHUNK

    def chunk_rows(c):
        return pl.ds(pl.multiple_of(c * CHUNK, CHUNK), CHUNK)

    def local_part(c):
        rows = chunk_rows(c)
        f = lb + (1.0 - lb) * _sigmoid(f_ref[rows, :].astype(F32))
        k_all = 1.0 - f
        g_all = jnp.log(f) * LOG2E
        sh = 1
        while sh < CHUNK:
            g_all = g_all + jnp.where(row >= sh, pltpu.roll(g_all, sh, axis=0), 0.0)
            sh *= 2
        q_all = _silu(q_ref[rows, :].astype(F32))
        go_all = _silu(go_ref[rows, :].astype(F32))
        a = [None] * HG_HEADS
        for h in HG_LEVELS:
            d = jnp.exp2(-jnp.abs(g_all - _hgrn_pivot(g_all, h, row)))
            t = (jnp.where((row & h) != 0, q_all, k_all) * d).astype(BF16)
            keep = ((row_a // (2 * h)) == (col_a // (2 * h))) & ((row_a & h) != 0) & ((col_a & h) == 0)
            for hd, sl in enumerate(heads):
                prod = lax.dot_general(t[:, sl], t[:, sl], nt, preferred_element_type=F32)
                a[hd] = jnp.where(keep, prod, 0.0 if a[hd] is None else a[hd])
        qk = q_all * k_all
        for hd, sl in enumerate(heads):
            a[hd] = jnp.where(row_a == col_a, jnp.sum(qk[:, sl], axis=-1, keepdims=True), a[hd]).astype(BF16)

        g_last = g_all[CHUNK - 1:CHUNK]
        q_in = (q_all * jnp.exp2(g_all)).astype(BF16)
        k_up = (k_all * jnp.exp2(g_last - g_all)).astype(BF16)
        return a, q_in, k_up, go_all, jnp.exp2(g_last)

    def state_part(c, vals):
        rows = chunk_rows(c)
        i_all = i_ref[rows, :]
        a, q_in, k_up, go_all, dec = vals
        for hd, sl in enumerate(heads):
            st = st_ref[hd]
            o = jnp.dot(jnp.concatenate([q_in[:, sl], a[hd]], axis=1),
                        jnp.concatenate([st.T.astype(BF16), i_all[:, sl]], axis=0),
                        preferred_element_type=F32)
            i_t = i_all[:, sl].astype(F32).T.astype(BF16)
            st_ref[hd] = st * dec[:, sl] + jnp.dot(i_t, k_up[:, sl], preferred_element_type=F32)
            o_ref[rows, sl] = (_rms_rows(o, ng) * go_all[:, sl]).astype(o_ref.dtype)

    def step(k, carry):
        c = 2 * k
        first = local_part(c)
        second = local_part(c + 1)
        state_part(c, first)
        state_part(c + 1, second)
        return carry

    assert n_chunks % 2 == 0
    lax.fori_loop(0, n_chunks // 2, step, 0)


def _hgrn(zhg, lb_logits, norm_g, batch, seq):
    zhg3 = zhg.reshape(batch, seq, 4 * HG_WIDTH)
    blk = lambda off: pl.BlockSpec((None, seq, HG_WIDTH), lambda b: (b, 0, off))
    return pl.pallas_call(
        functools.partial(_hgrn_kernel, seq=seq),
        grid=(batch,),
        in_specs=[blk(0), blk(1), blk(2), blk(3), _const_spec(lb_logits.shape), _const_spec(norm_g.shape)],
        out_specs=pl.BlockSpec((None, seq, HG_WIDTH), lambda b: (b, 0, 0)),
        out_shape=jax.ShapeDtypeStruct((batch, seq, HG_WIDTH), BF16),
        scratch_shapes=[pltpu.VMEM((HG_HEADS, HG_D, HG_D), F32)],
        compiler_params=_params(("parallel",)),
        name="hgrn",
    )(zhg3, zhg3, zhg3, zhg3, lb_logits, norm_g)


FOX_AUG = FOX_DH
FOX_ROWS = 16
FOX_NAUG = 6


def _split3(x):
    hi = x.astype(BF16).astype(F32)
    r = x - hi
    lo = r.astype(BF16).astype(F32)
    return hi, lo, r - lo


def _foxprep_kernel(q_ref, k_ref, v_ref, ff_ref, fb_ref, qg_ref, kg_ref, bd_ref, place_ref,
                    qa_ref, ka_ref, vt_ref, *, seq):
    for p in range(FOX_HEADS // 2):
        vt_ref[p] = v_ref[:, p * LANES:(p + 1) * LANES].astype(F32).T.astype(BF16)
    x = ff_ref[...] + fb_ref[...]
    fc = -(jnp.maximum(-x, 0.0) + jnp.log1p(jnp.exp(-jnp.abs(x))))
    tok = lax.broadcasted_iota(jnp.int32, (FOX_ROWS, seq), 1)
    sh = 1
    while sh < seq:
        fc = fc + jnp.where(tok >= sh, pltpu.roll(fc, sh, axis=1), 0.0)
        sh *= 2
    fc = fc * LOG2E

    def normed(ref, g_ref, scale):
        v = ref[...].astype(F32)
        sq = (v * v).astype(BF16)
        w = bd_ref.shape[0]
        ms = jnp.concatenate([jnp.dot(sq[:, c:c + w], bd_ref[...], preferred_element_type=F32)
                              for c in range(0, FOX_WIDTH, w)], axis=1)
        return v * lax.rsqrt(ms + EPS) * (g_ref[...] * scale)

    qn = normed(q_ref, qg_ref, FOX_DH ** -0.5 * LOG2E)
    kn = normed(k_ref, kg_ref, 1.0)
    hi, lo, lo2 = _split3(fc)
    parts = jnp.concatenate([hi, lo, lo2, jnp.ones_like(hi)], axis=0)
    aug = jnp.dot(parts.T.astype(BF16), place_ref[...],
                  preferred_element_type=F32)
    lane = lax.broadcasted_iota(jnp.int32, (seq, LANES), 1)
    for h in range(FOX_HEADS):
        aug_q = pltpu.roll(aug, (FOX_AUG - FOX_NAUG * h) % LANES, axis=1)
        aug_k = pltpu.roll(aug, (FOX_AUG - FOX_DH - FOX_NAUG * h) % LANES, axis=1) if h else aug
        pair = slice(LANES * (h // 2), LANES * (h // 2) + LANES)
        pq, pk = qn[:, pair], kn[:, pair]
        if h % 2:
            pq = pltpu.roll(pq, FOX_DH, axis=1)
            pk = pltpu.roll(pk, FOX_DH, axis=1)
        qa_ref[h] = jnp.where(lane < FOX_DH, pq, aug_q).astype(BF16)
        ka_ref[h] = jnp.where(lane < FOX_DH, pk, jnp.where(lane < FOX_AUG + FOX_NAUG, aug_k, 0.0)).astype(BF16)


def _fox_place_matrix():
    place = np.zeros((4 * FOX_ROWS, LANES), np.float32)
    for h in range(FOX_HEADS):
        for part in range(3):
            place[part * FOX_ROWS + h, FOX_NAUG * h + part] = 1.0
            place[part * FOX_ROWS + h, FOX_DH + FOX_NAUG * h + 3 + part] = -1.0
            place[3 * FOX_ROWS, FOX_NAUG * h + 3 + part] = 1.0
            place[3 * FOX_ROWS, FOX_DH + FOX_NAUG * h + part] = 1.0
    return jnp.asarray(place, BF16)


def _fox_prep(zfox3, zfft, fb, qg, kg, bd, batch, seq):
    place = _fox_place_matrix()
    head_out = pl.BlockSpec((None, FOX_HEADS, seq, LANES), lambda b: (b, 0, 0, 0))
    return pl.pallas_call(
        functools.partial(_foxprep_kernel, seq=seq),
        grid=(batch,),
        in_specs=[pl.BlockSpec((None, seq, FOX_WIDTH), lambda b: (b, 0, 0)),
                  pl.BlockSpec((None, seq, FOX_WIDTH), lambda b: (b, 0, 1)),
                  pl.BlockSpec((None, seq, FOX_WIDTH), lambda b: (b, 0, 2)),
                  pl.BlockSpec((FOX_ROWS, seq), lambda b: (0, b)),
                  _const_spec(fb.shape), _const_spec(qg.shape), _const_spec(kg.shape), _const_spec(bd.shape),
                  _const_spec(place.shape)],
        out_specs=[head_out, head_out,
                   pl.BlockSpec((None, FOX_HEADS // 2, LANES, seq), lambda b: (b, 0, 0, 0))],
        out_shape=[jax.ShapeDtypeStruct((batch, FOX_HEADS, seq, LANES), BF16)] * 2
                  + [jax.ShapeDtypeStruct((batch, FOX_HEADS // 2, LANES, seq), BF16)],
        compiler_params=_params(("parallel",)),
        name="fox_prep",
    )(zfox3, zfox3, zfox3, zfft, fb, qg, kg, bd, place)


FOX_FAST_LOGIT_BOUND = 100.0


def _fox_fast(qa_ref, ka_ref, vt_ref, o_ref, tq, n_blk):
    key = lax.broadcasted_iota(jnp.int32, (tq, tq), 0)
    qry = lax.broadcasted_iota(jnp.int32, (tq, tq), 1)
    nt = (((1,), (1,)), ((), ()))
    for i in range(n_blk):
        l = [None, None]
        acc = [None, None]
        for j in range(i + 1):
            ss = [lax.dot_general(ka_ref[hh, j * tq:(j + 1) * tq, :], qa_ref[hh, i * tq:(i + 1) * tq, :],
                                  nt, preferred_element_type=F32) for hh in range(2)]
            for hh in range(2):
                s = jnp.where(key <= qry, ss[hh], NEG_BIG) if j == i else ss[hh]
                p = jnp.exp2(s)
                vt = vt_ref[hh * FOX_DH:(hh + 1) * FOX_DH, j * tq:(j + 1) * tq]
                ps = jnp.sum(p, axis=0, keepdims=True)
                pv = jnp.dot(vt, p.astype(BF16), preferred_element_type=F32)
                l[hh] = ps if j == 0 else l[hh] + ps
                acc[hh] = pv if j == 0 else acc[hh] + pv
        out_t = jnp.concatenate([acc[hh] / l[hh] for hh in range(2)], axis=0)
        o_ref[i * tq:(i + 1) * tq, :] = out_t.T.astype(o_ref.dtype)


def _foxattn_kernel(bound_ref, qa_ref, ka_ref, vt_ref, o_ref, s_ref, m_ref, l_ref, acc_ref, *, tq, n_blk):
    fast = bound_ref[0] <= FOX_FAST_LOGIT_BOUND

    @pl.when(fast)
    def _():
        _fox_fast(qa_ref, ka_ref, vt_ref, o_ref, tq, n_blk)

    @pl.when(jnp.logical_not(fast))
    def _():
        _fox_online(qa_ref, ka_ref, vt_ref, o_ref, s_ref, m_ref, l_ref, acc_ref, tq, n_blk)


def _fox_online(qa_ref, ka_ref, vt_ref, o_ref, s_ref, m_ref, l_ref, acc_ref, tq, n_blk):
    key = lax.broadcasted_iota(jnp.int32, (tq, tq), 0)
    qry = lax.broadcasted_iota(jnp.int32, (tq, tq), 1)
    nt = (((1,), (1,)), ((), ()))
    steps = [(i, j) for i in range(n_blk) for j in range(i + 1)]

    def scores(n, hh):
        i, j = steps[n]
        s_ref[n % 2, hh] = lax.dot_general(ka_ref[hh, j * tq:(j + 1) * tq, :], qa_ref[hh, i * tq:(i + 1) * tq, :],
                                           nt, preferred_element_type=F32)

    def reduce(n, hh):
        i, j = steps[n]
        s = s_ref[n % 2, hh]
        if j == i:
            s = jnp.where(key <= qry, s, NEG_BIG)
        m_new = jnp.max(s, axis=0, keepdims=True)
        vt = vt_ref[hh * FOX_DH:(hh + 1) * FOX_DH, j * tq:(j + 1) * tq]
        if j == 0:
            p = jnp.exp2(s - m_new)
            m_ref[hh] = m_new
            l_ref[hh] = jnp.sum(p, axis=0, keepdims=True)
            acc_ref[hh] = jnp.dot(vt, p.astype(BF16), preferred_element_type=F32)
        else:
            m = m_ref[hh]
            m_new = jnp.maximum(m, m_new)
            alpha = jnp.exp2(m - m_new)
            p = jnp.exp2(s - m_new)
            m_ref[hh] = m_new
            l_ref[hh] = alpha * l_ref[hh] + jnp.sum(p, axis=0, keepdims=True)
            acc_ref[hh] = alpha * acc_ref[hh] + jnp.dot(vt, p.astype(BF16), preferred_element_type=F32)

    for hh in range(2):
        scores(0, hh)
    for n, (i, j) in enumerate(steps):
        for hh in range(2):
            if n + 1 < len(steps):
                scores(n + 1, hh)
            reduce(n, hh)
        if j == i:
            out_t = jnp.concatenate([acc_ref[hh] / l_ref[hh] for hh in range(2)], axis=0)
            o_ref[i * tq:(i + 1) * tq, :] = out_t.T.astype(o_ref.dtype)


def _fox_attn(qk_bound, qa, ka, vt, batch, seq, tq):
    pair = lambda b, p: (b, p, 0, 0)
    return pl.pallas_call(
        functools.partial(_foxattn_kernel, tq=tq, n_blk=seq // tq),
        grid=(batch, FOX_HEADS // 2),
        in_specs=[pl.BlockSpec(memory_space=pltpu.SMEM),
                  pl.BlockSpec((None, 2, seq, LANES), pair), pl.BlockSpec((None, 2, seq, LANES), pair),
                  pl.BlockSpec((None, None, LANES, seq), pair)],
        out_specs=pl.BlockSpec((None, seq, LANES), lambda b, p: (b, 0, p)),
        out_shape=jax.ShapeDtypeStruct((batch, seq, FOX_WIDTH), BF16),
        scratch_shapes=[pltpu.VMEM((2, 2, tq, tq), F32), pltpu.VMEM((2, 1, tq), F32), pltpu.VMEM((2, 1, tq), F32),
                        pltpu.VMEM((2, FOX_DH, tq), F32)],
        compiler_params=_params(("parallel", "parallel")),
        name="fox_attn",
    )(qk_bound, qa, ka, vt)


def _memkv_kernel(m_ref, g_ref, w_ref, kg_ref, k_ref, v_ref):
    h = _rms_rows(m_ref[...], g_ref[...]).astype(BF16)
    kv = jnp.dot(h, w_ref[...], preferred_element_type=F32)
    for hd in range(MEM_HEADS):
        sl = slice(hd * MEM_DH, (hd + 1) * MEM_DH)
        k_ref[:, sl] = _rms_rows(kv[:, sl], kg_ref[...]).astype(k_ref.dtype)
    v_ref[...] = kv[:, MEM_WIDTH:].astype(v_ref.dtype)


def _mem_kv(mem2, g, w, kg, tm):
    n_rows = mem2.shape[0]
    row = lambda i: (i, 0)
    return pl.pallas_call(
        _memkv_kernel,
        grid=(n_rows // tm,),
        in_specs=[pl.BlockSpec((tm, D_MODEL), row), _const_spec(g.shape), _const_spec(w.shape),
                  _const_spec(kg.shape)],
        out_specs=[pl.BlockSpec((tm, MEM_WIDTH), row)] * 2,
        out_shape=[jax.ShapeDtypeStruct((n_rows, MEM_WIDTH), BF16)] * 2,
        compiler_params=_params(("parallel",)),
        name="mem_kv",
    )(mem2, g, w, kg)


def _memattn_kernel(q_ref, k_ref, v_ref, qg_ref, o_ref):
    for hd in range(MEM_HEADS):
        sl = slice(hd * MEM_DH, (hd + 1) * MEM_DH)
        q = _rms_rows(q_ref[:, sl].astype(F32), qg_ref[...] * (MEM_DH ** -0.5)).astype(BF16)
        s = lax.dot_general(q, k_ref[:, sl], (((1,), (1,)), ((), ())), preferred_element_type=F32)
        p = jnp.exp(s - jnp.max(s, axis=-1, keepdims=True))
        l = jnp.sum(p, axis=-1, keepdims=True)
        o = jnp.dot(p.astype(BF16), v_ref[:, sl], preferred_element_type=F32)
        o_ref[:, sl] = (o / l).astype(o_ref.dtype)


def _mem_attn(zmq3, mk3, mv3, qg, batch, seq, tq):
    n_mem = mk3.shape[1]
    kv_spec = pl.BlockSpec((None, n_mem, MEM_WIDTH), lambda b, i: (b, 0, 0))
    return pl.pallas_call(
        _memattn_kernel,
        grid=(batch, seq // tq),
        in_specs=[pl.BlockSpec((None, tq, MEM_WIDTH), lambda b, i: (b, i, 0)), kv_spec, kv_spec,
                  _const_spec(qg.shape)],
        out_specs=pl.BlockSpec((None, tq, MEM_WIDTH), lambda b, i: (b, i, 0)),
        out_shape=jax.ShapeDtypeStruct((batch, seq, MEM_WIDTH), BF16),
        compiler_params=_params(("parallel", "parallel")),
        name="mem_attn",
    )(zmq3, mk3, mv3, qg)


def _mergeffn_kernel(x_ref, ya_ref, yb_ref, yc_ref, gt_ref, wba_ref, wbb_ref, wbc_ref, wo_ref, ng_ref,
                     wa_ref, wv_ref, cw_ref, cb_ref, wd_ref, o_ref, carry_ref, y_ref, *, tm, fc):
    @pl.when(pl.program_id(1) == 0)
    def _():
        carry_ref[...] = jnp.zeros_like(carry_ref)

    merged = None
    for br, (yb_r, w_ref) in enumerate(((ya_ref, wba_ref), (yb_ref, wbb_ref), (yc_ref, wbc_ref))):
        gate = _sigmoid(gt_ref[:, br * D_MODEL:(br + 1) * D_MODEL].astype(F32))
        t = gate * jnp.dot(yb_r[...], w_ref[...], preferred_element_type=F32)
        merged = t if merged is None else merged + t
    x1 = x_ref[...] + jnp.dot(merged.astype(BF16), wo_ref[...], preferred_element_type=F32)
    o_ref[...] = x1

    h = _rms_rows(x1, ng_ref[...]).astype(BF16)
    row = lax.broadcasted_iota(jnp.int32, (tm, fc), 0)
    for c in range(D_FF // fc):
        sl = slice(c * fc, (c + 1) * fc)
        a = jnp.dot(h, wa_ref[:, sl], preferred_element_type=F32)
        v = jnp.dot(h, wv_ref[:, sl], preferred_element_type=F32)
        prev = carry_ref[c]
        p1 = prev[SUBLANES - 1:SUBLANES]
        p2 = prev[SUBLANES - 2:SUBLANES - 1]
        a1 = jnp.where(row == 0, p1, pltpu.roll(a, 1, axis=0))
        a2 = jnp.where(row == 0, p2, jnp.where(row == 1, p1, pltpu.roll(a, 2, axis=0)))
        carry_ref[c] = a[tm - SUBLANES:tm]
        z = a2 * cw_ref[0:1, sl] + a1 * cw_ref[1:2, sl] + a * cw_ref[2:3, sl] + cb_ref[:, sl]
        gelu = 0.5 * z * (1.0 + lax.erf(z * (2.0 ** -0.5)))
        y_ref[:, sl] = (gelu * v).astype(y_ref.dtype)
    o_ref[...] += jnp.dot(y_ref[...], wd_ref[...], preferred_element_type=F32)


def _merge_ffn(x3, ya, yb, yc, zgate3, wba, wbb, wbc, wo, ng, wa, wv, cw, cb, wd, tm, fc):
    batch, seq, _ = x3.shape
    tile = lambda w: pl.BlockSpec((None, tm, w), lambda b, t: (b, t, 0))
    consts = (wba, wbb, wbc, wo, ng, wa, wv, cw, cb, wd)
    return pl.pallas_call(
        functools.partial(_mergeffn_kernel, tm=tm, fc=fc),
        grid=(batch, seq // tm),
        in_specs=[tile(D_MODEL), tile(HG_WIDTH), tile(FOX_WIDTH), tile(MEM_WIDTH), tile(N_BRANCH * D_MODEL)]
                 + [_const_spec(c.shape) for c in consts],
        out_specs=tile(D_MODEL),
        out_shape=jax.ShapeDtypeStruct((batch, seq, D_MODEL), F32),
        scratch_shapes=[pltpu.VMEM((D_FF // fc, SUBLANES, fc), F32), pltpu.VMEM((tm, D_FF), BF16)],
        compiler_params=_params(("parallel", "arbitrary")),
        name="merge_ffn",
    )(x3, ya, yb, yc, zgate3, *consts)


def kernel(x, mem, norm_mix_g, norm_mem_g, w_in, hgrn_lb_logits, hgrn_norm_g, fox_f_bias, fox_q_norm_g,
           fox_k_norm_g, mem_kv_w, mem_q_norm_g, mem_k_norm_g, w_br_hgrn, w_br_fox, w_br_mem, w_out,
           norm_ffn_g, ffn_w_up, ffn_conv_w, ffn_conv_b, ffn_w_down):
    batch, seq, _ = x.shape
    n_mem = mem.shape[1]
    depth = w_in.shape[0]
    assert depth == 1 and seq % 512 == 0

    x2 = x.reshape(batch * seq, D_MODEL)
    for l in range(depth):
        w = w_in[l]
        c_hg, c_fox = 4 * HG_WIDTH, 3 * FOX_WIDTH
        o_ff = c_hg + c_fox
        o_mq = o_ff + FOX_HEADS
        o_gate = o_mq + MEM_WIDTH
        whg = w[:, :c_hg].astype(BF16)
        wfox = w[:, c_hg:o_ff].astype(BF16)
        wfft = jnp.pad(w[:, o_ff:o_mq].T, ((0, FOX_ROWS - FOX_HEADS), (0, 0))).astype(BF16)
        wmq = w[:, o_mq:o_gate].astype(BF16)
        wgate = w[:, o_gate:].astype(BF16)
        row = lambda v: v.reshape(1, -1).astype(F32)

        zhg, zfox, zmq, zgate, zfft = _in_proj(x2, row(norm_mix_g[l]), whg, wfox, wmq, wgate, wfft, tm=512)

        ya = _hgrn(zhg, hgrn_lb_logits.astype(F32), row(hgrn_norm_g[l]), batch, seq)

        zfox3 = zfox.reshape(batch, seq, c_fox)
        fb = jnp.pad(fox_f_bias[l].astype(F32), (0, FOX_ROWS - FOX_HEADS)).reshape(FOX_ROWS, 1)
        head_id = jnp.arange(MXU_WIDTH) // FOX_DH
        bd = (head_id[:, None] == head_id[None, :]).astype(BF16) * (1.0 / FOX_DH)
        qa, ka, vt = _fox_prep(zfox3, zfft, fb,
                               jnp.tile(row(fox_q_norm_g[l]), (1, FOX_HEADS)),
                               jnp.tile(row(fox_k_norm_g[l]), (1, FOX_HEADS)), bd, batch, seq)
        qk_bound = (LOG2E * FOX_DH ** 0.5 * jnp.max(jnp.abs(fox_q_norm_g[l])) * jnp.max(jnp.abs(fox_k_norm_g[l])))
        yb = _fox_attn(qk_bound.reshape(1).astype(F32), qa, ka, vt, batch, seq, tq=512)

        mk, mv = _mem_kv(mem.reshape(batch * n_mem, D_MODEL), row(norm_mem_g[l]), mem_kv_w[l].astype(BF16),
                         row(mem_k_norm_g[l]), tm=512)
        yc = _mem_attn(zmq.reshape(batch, seq, MEM_WIDTH), mk.reshape(batch, n_mem, MEM_WIDTH),
                       mv.reshape(batch, n_mem, MEM_WIDTH), row(mem_q_norm_g[l]), batch, seq, tq=512)

        wup = ffn_w_up[l]
        out = _merge_ffn(x2.reshape(batch, seq, D_MODEL), ya, yb, yc, zgate.reshape(batch, seq, N_BRANCH * D_MODEL),
                         w_br_hgrn[l].astype(BF16), w_br_fox[l].astype(BF16), w_br_mem[l].astype(BF16),
                         w_out[l].astype(BF16), row(norm_ffn_g[l]), wup[:, :D_FF].astype(BF16),
                         wup[:, D_FF:].astype(BF16), ffn_conv_w[l].astype(F32), row(ffn_conv_b[l]),
                         ffn_w_down[l].astype(BF16), tm=512, fc=256)
        x2 = out.reshape(batch * seq, D_MODEL)
    return x2.reshape(batch, seq, D_MODEL)
```

```python
import functools

import numpy as np
import jax
import jax.numpy as jnp
from jax import lax
from jax.experimental import pallas as pl
from jax.experimental.pallas import tpu as pltpu

F32 = jnp.float32
BF16 = jnp.bfloat16

D_MODEL = 1024
EPS = 1e-6
CHUNK = 64
HG_HEADS = 4
HG_D = 128
HG_WIDTH = HG_HEADS * HG_D
FOX_HEADS = 8
FOX_DH = 64
FOX_WIDTH = FOX_HEADS * FOX_DH
MEM_HEADS = 4
MEM_DH = 128
MEM_WIDTH = MEM_HEADS * MEM_DH
N_BRANCH = 3
D_FF = 2816
CONV_W = 3

LANES = 128
SUBLANES = 8
MXU_WIDTH = 256
VMEM_LIMIT = 56 * 1024 * 1024

NEG_BIG = -1e30
LOG2E = 1.4426950408889634


def _const_spec(shape):
    nd = len(shape)
    return pl.BlockSpec(shape, lambda *_: (0,) * nd, pipeline_mode=pl.Buffered(1))


def _params(sem):
    return pltpu.CompilerParams(dimension_semantics=sem, vmem_limit_bytes=VMEM_LIMIT)


def _rms_rows(x, g):
    return x * lax.rsqrt(jnp.mean(x * x, axis=-1, keepdims=True) + EPS) * g


def _sigmoid(x):
    return 1.0 / (1.0 + jnp.exp(-x))


def _silu(x):
    return x * _sigmoid(x)


def _inproj_kernel(x_ref, g_ref, whg_ref, wfox_ref, wmq_ref, wgate_ref, wfft_ref, mqg_ref,
                   zhg_ref, zfox_ref, zmq_ref, zgate_ref, zfft_ref, *, tn):
    h = _rms_rows(x_ref[...], g_ref[...]).astype(BF16)
    for w_ref, o_ref in ((whg_ref, zhg_ref), (wfox_ref, zfox_ref), (wgate_ref, zgate_ref)):
        n = w_ref.shape[1]
        step = min(tn, n)
        for c in range(0, n, step):
            o_ref[:, c:c + step] = jnp.dot(h, w_ref[:, c:c + step],
                                           preferred_element_type=F32).astype(o_ref.dtype)
    zq = jnp.dot(h, wmq_ref[...], preferred_element_type=F32)
    gain = mqg_ref[...] * (MEM_DH ** -0.5 * LOG2E)
    for hd in range(MEM_HEADS):
        sl = slice(hd * MEM_DH, (hd + 1) * MEM_DH)
        zmq_ref[:, sl] = _rms_rows(zq[:, sl], gain).astype(zmq_ref.dtype)
    zfft_ref[...] = lax.dot_general(wfft_ref[...], h, (((1,), (1,)), ((), ())), preferred_element_type=F32)


def _in_proj(x2, g, whg, wfox, wmq, wgate, wfft, mqg, tm):
    n_rows = x2.shape[0]
    ws = (whg, wfox, wmq, wgate)
    row = lambda i: (i, 0)
    return pl.pallas_call(
        functools.partial(_inproj_kernel, tn=512),
        grid=(n_rows // tm,),
        in_specs=[pl.BlockSpec((tm, D_MODEL), row), _const_spec(g.shape)]
                 + [_const_spec(w.shape) for w in ws] + [_const_spec(wfft.shape), _const_spec(mqg.shape)],
        out_specs=[pl.BlockSpec((tm, w.shape[1]), row) for w in ws]
                  + [pl.BlockSpec((wfft.shape[0], tm), lambda i: (0, i))],
        out_shape=[jax.ShapeDtypeStruct((n_rows, w.shape[1]), BF16) for w in ws]
                  + [jax.ShapeDtypeStruct((wfft.shape[0], n_rows), F32)],
        compiler_params=_params(("parallel",)),
        name="in_proj",
    )(x2, g, *ws, wfft, mqg)


HG_LEVELS = (32, 16, 8, 4, 2, 1)


def _hgrn_pivot(g, h, row):
    n, w = g.shape
    if h >= SUBLANES:
        parts = [jnp.broadcast_to(g[p:p + 1], (2 * h, w)) for p in range(h - 1, n, 2 * h)]
        return parts[0] if len(parts) == 1 else jnp.concatenate(parts, axis=0)
    if h == 1:
        return jnp.where((row & 1) != 0, pltpu.roll(g, 1, axis=0), g)
    parts = []
    row8 = lax.broadcasted_iota(jnp.int32, (SUBLANES, w), 0)
    for lo in range(0, n, SUBLANES):
        cands = [jnp.broadcast_to(g[p:p + 1], (SUBLANES, w)) for p in range(lo + h - 1, lo + SUBLANES, 2 * h)]
        piv = cands[-1]
        for j in range(len(cands) - 2, -1, -1):
            piv = jnp.where(row8 < (j + 1) * 2 * h, cands[j], piv)
        parts.append(piv)
    return jnp.concatenate(parts, axis=0)


def _hgrn_kernel(q_ref, f_ref, i_ref, go_ref, lbl_ref, ng_ref, o_ref, st_ref, *, seq):
    lbl = lbl_ref[...]
    e = jnp.exp(lbl - jnp.max(lbl, axis=0, keepdims=True))
    lb = e[0:1] / jnp.sum(e, axis=0, keepdims=True)
    ng = ng_ref[...]
    st_ref[...] = jnp.zeros_like(st_ref)

    row = lax.broadcasted_iota(jnp.int32, (CHUNK, HG_WIDTH), 0)
    row_a = lax.broadcasted_iota(jnp.int32, (CHUNK, CHUNK), 0)
    col_a = lax.broadcasted_iota(jnp.int32, (CHUNK, CHUNK), 1)
    nt = (((1,), (1,)), ((), ()))
    heads = [slice(hd * HG_D, (hd + 1) * HG_D) for hd in range(HG_HEADS)]
    n_chunks = seq // CHUNK

    def chunk_rows(c):
        return pl.ds(pl.multiple_of(c * CHUNK, CHUNK), CHUNK)

    def local_part(c):
        rows = chunk_rows(c)
        f = lb + (1.0 - lb) * _sigmoid(f_ref[rows, :].astype(F32))
        k_all = 1.0 - f
        g_all = jnp.log(f) * LOG2E
        sh = 1
        while sh < CHUNK:
            g_all = g_all + jnp.where(row >= sh, pltpu.roll(g_all, sh, axis=0), 0.0)
            sh *= 2
        q_all = _silu(q_ref[rows, :].astype(F32))
        go_all = _silu(go_ref[rows, :].astype(F32))
        a = [None] * HG_HEADS
        for h in HG_LEVELS:
            d = jnp.exp2(-jnp.abs(g_all - _hgrn_pivot(g_all, h, row)))
            t = (jnp.where((row & h) != 0, q_all, k_all) * d).astype(BF16)
            keep = ((row_a // (2 * h)) == (col_a // (2 * h))) & ((row_a & h) != 0) & ((col_a & h) == 0)
            for hd, sl in enumerate(heads):
                prod = lax.dot_general(t[:, sl], t[:, sl], nt, preferred_element_type=F32)
                a[hd] = jnp.where(keep, prod, 0.0 if a[hd] is None else a[hd])
        qk = q_all * k_all
        for hd, sl in enumerate(heads):
            a[hd] = jnp.where(row_a == col_a, jnp.sum(qk[:, sl], axis=-1, keepdims=True), a[hd]).astype(BF16)

        g_last = g_all[CHUNK - 1:CHUNK]
        q_in = (q_all * jnp.exp2(g_all)).astype(BF16)
        k_up = (k_all * jnp.exp2(g_last - g_all)).astype(BF16)
        return a, q_in, k_up, go_all, jnp.exp2(g_last)

    def state_part(c, vals):
        rows = chunk_rows(c)
        i_all = i_ref[rows, :]
        a, q_in, k_up, go_all, dec = vals
        for hd, sl in enumerate(heads):
            st = st_ref[hd]
            o = jnp.dot(jnp.concatenate([q_in[:, sl], a[hd]], axis=1),
                        jnp.concatenate([st.T.astype(BF16), i_all[:, sl]], axis=0),
                        preferred_element_type=F32)
            i_t = i_all[:, sl].astype(F32).T.astype(BF16)
            st_ref[hd] = st * dec[:, sl] + jnp.dot(i_t, k_up[:, sl], preferred_element_type=F32)
            o_ref[rows, sl] = (_rms_rows(o, ng) * go_all[:, sl]).astype(o_ref.dtype)

    def step(k, carry):
        c = 2 * k
        first = local_part(c)
        second = local_part(c + 1)
        state_part(c, first)
        state_part(c + 1, second)
        return carry

    assert n_chunks % 2 == 0
    lax.fori_loop(0, n_chunks // 2, step, 0)


def _hgrn(zhg, lb_logits, norm_g, batch, seq):
    zhg3 = zhg.reshape(batch, seq, 4 * HG_WIDTH)
    blk = lambda off: pl.BlockSpec((None, seq, HG_WIDTH), lambda b: (b, 0, off))
    return pl.pallas_call(
        functools.partial(_hgrn_kernel, seq=seq),
        grid=(batch,),
        in_specs=[blk(0), blk(1), blk(2), blk(3), _const_spec(lb_logits.shape), _const_spec(norm_g.shape)],
        out_specs=pl.BlockSpec((None, seq, HG_WIDTH), lambda b: (b, 0, 0)),
        out_shape=jax.ShapeDtypeStruct((batch, seq, HG_WIDTH), BF16),
        scratch_shapes=[pltpu.VMEM((HG_HEADS, HG_D, HG_D), F32)],
        compiler_params=_params(("parallel",)),
        name="hgrn",
    )(zhg3, zhg3, zhg3, zhg3, lb_logits, norm_g)


FOX_AUG = FOX_DH
FOX_ROWS = 16
FOX_NAUG = 6


def _split3(x):
    hi = x.astype(BF16).astype(F32)
    r = x - hi
    lo = r.astype(BF16).astype(F32)
    return hi, lo, r - lo


def _foxprep_kernel(q_ref, k_ref, v_ref, ff_ref, fb_ref, qg_ref, kg_ref, bd_ref, place_ref,
                    qa_ref, ka_ref, vt_ref, *, seq):
    for p in range(FOX_HEADS // 2):
        vt_ref[p] = v_ref[:, p * LANES:(p + 1) * LANES].astype(F32).T.astype(BF16)
    x = ff_ref[...] + fb_ref[...]
    fc = -(jnp.maximum(-x, 0.0) + jnp.log1p(jnp.exp(-jnp.abs(x))))
    tok = lax.broadcasted_iota(jnp.int32, (FOX_ROWS, seq), 1)
    sh = 1
    while sh < seq:
        fc = fc + jnp.where(tok >= sh, pltpu.roll(fc, sh, axis=1), 0.0)
        sh *= 2
    fc = fc * LOG2E

    def normed(ref, g_ref, scale):
        v = ref[...].astype(F32)
        sq = (v * v).astype(BF16)
        w = bd_ref.shape[0]
        ms = jnp.concatenate([jnp.dot(sq[:, c:c + w], bd_ref[...], preferred_element_type=F32)
                              for c in range(0, FOX_WIDTH, w)], axis=1)
        return v * lax.rsqrt(ms + EPS) * (g_ref[...] * scale)

    qn = normed(q_ref, qg_ref, FOX_DH ** -0.5 * LOG2E)
    kn = normed(k_ref, kg_ref, 1.0)
    hi, lo, lo2 = _split3(fc)
    parts = jnp.concatenate([hi, lo, lo2, jnp.ones_like(hi)], axis=0)
    aug = jnp.dot(parts.T.astype(BF16), place_ref[...],
                  preferred_element_type=F32)
    lane = lax.broadcasted_iota(jnp.int32, (seq, LANES), 1)
    for h in range(FOX_HEADS):
        aug_q = pltpu.roll(aug, (FOX_AUG - FOX_NAUG * h) % LANES, axis=1)
        aug_k = pltpu.roll(aug, (FOX_AUG - FOX_DH - FOX_NAUG * h) % LANES, axis=1) if h else aug
        pair = slice(LANES * (h // 2), LANES * (h // 2) + LANES)
        pq, pk = qn[:, pair], kn[:, pair]
        if h % 2:
            pq = pltpu.roll(pq, FOX_DH, axis=1)
            pk = pltpu.roll(pk, FOX_DH, axis=1)
        qa_ref[h] = jnp.where(lane < FOX_DH, pq, aug_q).astype(BF16)
        ka_ref[h] = jnp.where(lane < FOX_DH, pk, jnp.where(lane < FOX_AUG + FOX_NAUG, aug_k, 0.0)).astype(BF16)


def _fox_place_matrix():
    place = np.zeros((4 * FOX_ROWS, LANES), np.float32)
    for h in range(FOX_HEADS):
        for part in range(3):
            place[part * FOX_ROWS + h, FOX_NAUG * h + part] = 1.0
            place[part * FOX_ROWS + h, FOX_DH + FOX_NAUG * h + 3 + part] = -1.0
            place[3 * FOX_ROWS, FOX_NAUG * h + 3 + part] = 1.0
            place[3 * FOX_ROWS, FOX_DH + FOX_NAUG * h + part] = 1.0
    return jnp.asarray(place, BF16)


def _fox_prep(zfox3, zfft, fb, qg, kg, bd, batch, seq):
    place = _fox_place_matrix()
    head_out = pl.BlockSpec((None, FOX_HEADS, seq, LANES), lambda b: (b, 0, 0, 0))
    return pl.pallas_call(
        functools.partial(_foxprep_kernel, seq=seq),
        grid=(batch,),
        in_specs=[pl.BlockSpec((None, seq, FOX_WIDTH), lambda b: (b, 0, 0)),
                  pl.BlockSpec((None, seq, FOX_WIDTH), lambda b: (b, 0, 1)),
                  pl.BlockSpec((None, seq, FOX_WIDTH), lambda b: (b, 0, 2)),
                  pl.BlockSpec((FOX_ROWS, seq), lambda b: (0, b)),
                  _const_spec(fb.shape), _const_spec(qg.shape), _const_spec(kg.shape), _const_spec(bd.shape),
                  _const_spec(place.shape)],
        out_specs=[head_out, head_out,
                   pl.BlockSpec((None, FOX_HEADS // 2, LANES, seq), lambda b: (b, 0, 0, 0))],
        out_shape=[jax.ShapeDtypeStruct((batch, FOX_HEADS, seq, LANES), BF16)] * 2
                  + [jax.ShapeDtypeStruct((batch, FOX_HEADS // 2, LANES, seq), BF16)],
        compiler_params=_params(("parallel",)),
        name="fox_prep",
    )(zfox3, zfox3, zfox3, zfft, fb, qg, kg, bd, place)


FOX_FAST_LOGIT_BOUND = 100.0


def _fox_fast(qa_ref, ka_ref, vt_ref, o_ref, tq, n_blk):
    key = lax.broadcasted_iota(jnp.int32, (tq, tq), 0)
    qry = lax.broadcasted_iota(jnp.int32, (tq, tq), 1)
    nt = (((1,), (1,)), ((), ()))
    for i in range(n_blk):
        l = [None, None]
        acc = [None, None]
        for j in range(i + 1):
            ss = [lax.dot_general(ka_ref[hh, j * tq:(j + 1) * tq, :], qa_ref[hh, i * tq:(i + 1) * tq, :],
                                  nt, preferred_element_type=F32) for hh in range(2)]
            for hh in range(2):
                s = jnp.where(key <= qry, ss[hh], NEG_BIG) if j == i else ss[hh]
                p = jnp.exp2(s)
                vt = vt_ref[hh * FOX_DH:(hh + 1) * FOX_DH, j * tq:(j + 1) * tq]
                ps = jnp.sum(p, axis=0, keepdims=True)
                pv = jnp.dot(vt, p.astype(BF16), preferred_element_type=F32)
                l[hh] = ps if j == 0 else l[hh] + ps
                acc[hh] = pv if j == 0 else acc[hh] + pv
        out_t = jnp.concatenate([acc[hh] / l[hh] for hh in range(2)], axis=0)
        o_ref[i * tq:(i + 1) * tq, :] = out_t.T.astype(o_ref.dtype)


def _foxattn_kernel(bound_ref, qa_ref, ka_ref, vt_ref, o_ref, s_ref, m_ref, l_ref, acc_ref, *, tq, n_blk):
    fast = bound_ref[0] <= FOX_FAST_LOGIT_BOUND

    @pl.when(fast)
    def _():
        _fox_fast(qa_ref, ka_ref, vt_ref, o_ref, tq, n_blk)

    @pl.when(jnp.logical_not(fast))
    def _():
        _fox_online(qa_ref, ka_ref, vt_ref, o_ref, s_ref, m_ref, l_ref, acc_ref, tq, n_blk)


def _fox_online(qa_ref, ka_ref, vt_ref, o_ref, s_ref, m_ref, l_ref, acc_ref, tq, n_blk):
    key = lax.broadcasted_iota(jnp.int32, (tq, tq), 0)
    qry = lax.broadcasted_iota(jnp.int32, (tq, tq), 1)
    nt = (((1,), (1,)), ((), ()))
    steps = [(i, j) for i in range(n_blk) for j in range(i + 1)]

    def scores(n, hh):
        i, j = steps[n]
        s_ref[n % 2, hh] = lax.dot_general(ka_ref[hh, j * tq:(j + 1) * tq, :], qa_ref[hh, i * tq:(i + 1) * tq, :],
                                           nt, preferred_element_type=F32)

    def reduce(n, hh):
        i, j = steps[n]
        s = s_ref[n % 2, hh]
        if j == i:
            s = jnp.where(key <= qry, s, NEG_BIG)
        m_new = jnp.max(s, axis=0, keepdims=True)
        vt = vt_ref[hh * FOX_DH:(hh + 1) * FOX_DH, j * tq:(j + 1) * tq]
        if j == 0:
            p = jnp.exp2(s - m_new)
            m_ref[hh] = m_new
            l_ref[hh] = jnp.sum(p, axis=0, keepdims=True)
            acc_ref[hh] = jnp.dot(vt, p.astype(BF16), preferred_element_type=F32)
        else:
            m = m_ref[hh]
            m_new = jnp.maximum(m, m_new)
            alpha = jnp.exp2(m - m_new)
            p = jnp.exp2(s - m_new)
            m_ref[hh] = m_new
            l_ref[hh] = alpha * l_ref[hh] + jnp.sum(p, axis=0, keepdims=True)
            acc_ref[hh] = alpha * acc_ref[hh] + jnp.dot(vt, p.astype(BF16), preferred_element_type=F32)

    for hh in range(2):
        scores(0, hh)
    for n, (i, j) in enumerate(steps):
        for hh in range(2):
            if n + 1 < len(steps):
                scores(n + 1, hh)
            reduce(n, hh)
        if j == i:
            out_t = jnp.concatenate([acc_ref[hh] / l_ref[hh] for hh in range(2)], axis=0)
            o_ref[i * tq:(i + 1) * tq, :] = out_t.T.astype(o_ref.dtype)


def _fox_attn(qk_bound, qa, ka, vt, batch, seq, tq):
    pair = lambda b, p: (b, p, 0, 0)
    return pl.pallas_call(
        functools.partial(_foxattn_kernel, tq=tq, n_blk=seq // tq),
        grid=(batch, FOX_HEADS // 2),
        in_specs=[pl.BlockSpec(memory_space=pltpu.SMEM),
                  pl.BlockSpec((None, 2, seq, LANES), pair), pl.BlockSpec((None, 2, seq, LANES), pair),
                  pl.BlockSpec((None, None, LANES, seq), pair)],
        out_specs=pl.BlockSpec((None, seq, LANES), lambda b, p: (b, 0, p)),
        out_shape=jax.ShapeDtypeStruct((batch, seq, FOX_WIDTH), BF16),
        scratch_shapes=[pltpu.VMEM((2, 2, tq, tq), F32), pltpu.VMEM((2, 1, tq), F32), pltpu.VMEM((2, 1, tq), F32),
                        pltpu.VMEM((2, FOX_DH, tq), F32)],
        compiler_params=_params(("parallel", "parallel")),
        name="fox_attn",
    )(qk_bound, qa, ka, vt)


def _memkv_kernel(m_ref, g_ref, w_ref, kg_ref, k_ref, v_ref):
    h = _rms_rows(m_ref[...], g_ref[...]).astype(BF16)
    kv = jnp.dot(h, w_ref[...], preferred_element_type=F32)
    for hd in range(MEM_HEADS):
        sl = slice(hd * MEM_DH, (hd + 1) * MEM_DH)
        k_ref[:, sl] = _rms_rows(kv[:, sl], kg_ref[...]).astype(k_ref.dtype)
    v_ref[...] = kv[:, MEM_WIDTH:].astype(v_ref.dtype)


def _mem_kv(mem2, g, w, kg, tm):
    n_rows = mem2.shape[0]
    row = lambda i: (i, 0)
    return pl.pallas_call(
        _memkv_kernel,
        grid=(n_rows // tm,),
        in_specs=[pl.BlockSpec((tm, D_MODEL), row), _const_spec(g.shape), _const_spec(w.shape),
                  _const_spec(kg.shape)],
        out_specs=[pl.BlockSpec((tm, MEM_WIDTH), row)] * 2,
        out_shape=[jax.ShapeDtypeStruct((n_rows, MEM_WIDTH), BF16)] * 2,
        compiler_params=_params(("parallel",)),
        name="mem_kv",
    )(mem2, g, w, kg)


def _memattn_kernel(q_ref, k_ref, v_ref, o_ref):
    for hd in range(MEM_HEADS):
        sl = slice(hd * MEM_DH, (hd + 1) * MEM_DH)
        s = lax.dot_general(q_ref[:, sl], k_ref[:, sl], (((1,), (1,)), ((), ())), preferred_element_type=F32)
        p = jnp.exp2(s - jnp.max(s, axis=-1, keepdims=True))
        l = jnp.sum(p, axis=-1, keepdims=True)
        o = jnp.dot(p.astype(BF16), v_ref[:, sl], preferred_element_type=F32)
        o_ref[:, sl] = (o / l).astype(o_ref.dtype)


def _mem_attn(zmq3, mk3, mv3, batch, seq, tq):
    n_mem = mk3.shape[1]
    kv_spec = pl.BlockSpec((None, n_mem, MEM_WIDTH), lambda b, i: (b, 0, 0))
    return pl.pallas_call(
        _memattn_kernel,
        grid=(batch, seq // tq),
        in_specs=[pl.BlockSpec((None, tq, MEM_WIDTH), lambda b, i: (b, i, 0)), kv_spec, kv_spec],
        out_specs=pl.BlockSpec((None, tq, MEM_WIDTH), lambda b, i: (b, i, 0)),
        out_shape=jax.ShapeDtypeStruct((batch, seq, MEM_WIDTH), BF16),
        compiler_params=_params(("parallel", "parallel")),
        name="mem_attn",
    )(zmq3, mk3, mv3)


def _mergeffn_kernel(x_ref, ya_ref, yb_ref, yc_ref, gt_ref, wba_ref, wbb_ref, wbc_ref, wo_ref, ng_ref,
                     wa_ref, wv_ref, cw_ref, cb_ref, wd_ref, o_ref, carry_ref, y_ref, *, tm, fc):
    @pl.when(pl.program_id(1) == 0)
    def _():
        carry_ref[...] = jnp.zeros_like(carry_ref)

    merged = None
    for br, (yb_r, w_ref) in enumerate(((ya_ref, wba_ref), (yb_ref, wbb_ref), (yc_ref, wbc_ref))):
        gate = _sigmoid(gt_ref[:, br * D_MODEL:(br + 1) * D_MODEL].astype(F32))
        t = gate * jnp.dot(yb_r[...], w_ref[...], preferred_element_type=F32)
        merged = t if merged is None else merged + t
    x1 = x_ref[...] + jnp.dot(merged.astype(BF16), wo_ref[...], preferred_element_type=F32)
    o_ref[...] = x1

    h = _rms_rows(x1, ng_ref[...]).astype(BF16)
    row = lax.broadcasted_iota(jnp.int32, (tm, fc), 0)
    for c in range(D_FF // fc):
        sl = slice(c * fc, (c + 1) * fc)
        a = jnp.dot(h, wa_ref[:, sl], preferred_element_type=F32)
        v = jnp.dot(h, wv_ref[:, sl], preferred_element_type=F32)
        prev = carry_ref[c]
        p1 = prev[SUBLANES - 1:SUBLANES]
        p2 = prev[SUBLANES - 2:SUBLANES - 1]
        a1 = jnp.where(row == 0, p1, pltpu.roll(a, 1, axis=0))
        a2 = jnp.where(row == 0, p2, jnp.where(row == 1, p1, pltpu.roll(a, 2, axis=0)))
        carry_ref[c] = a[tm - SUBLANES:tm]
        z = a2 * cw_ref[0:1, sl] + a1 * cw_ref[1:2, sl] + a * cw_ref[2:3, sl] + cb_ref[:, sl]
        gelu = 0.5 * z * (1.0 + lax.erf(z * (2.0 ** -0.5)))
        y_ref[:, sl] = (gelu * v).astype(y_ref.dtype)
    o_ref[...] += jnp.dot(y_ref[...], wd_ref[...], preferred_element_type=F32)


def _merge_ffn(x3, ya, yb, yc, zgate3, wba, wbb, wbc, wo, ng, wa, wv, cw, cb, wd, tm, fc):
    batch, seq, _ = x3.shape
    tile = lambda w: pl.BlockSpec((None, tm, w), lambda b, t: (b, t, 0))
    consts = (wba, wbb, wbc, wo, ng, wa, wv, cw, cb, wd)
    return pl.pallas_call(
        functools.partial(_mergeffn_kernel, tm=tm, fc=fc),
        grid=(batch, seq // tm),
        in_specs=[tile(D_MODEL), tile(HG_WIDTH), tile(FOX_WIDTH), tile(MEM_WIDTH), tile(N_BRANCH * D_MODEL)]
                 + [_const_spec(c.shape) for c in consts],
        out_specs=tile(D_MODEL),
        out_shape=jax.ShapeDtypeStruct((batch, seq, D_MODEL), F32),
        scratch_shapes=[pltpu.VMEM((D_FF // fc, SUBLANES, fc), F32), pltpu.VMEM((tm, D_FF), BF16)],
        compiler_params=_params(("parallel", "arbitrary")),
        name="merge_ffn",
    )(x3, ya, yb, yc, zgate3, *consts)


def kernel(x, mem, norm_mix_g, norm_mem_g, w_in, hgrn_lb_logits, hgrn_norm_g, fox_f_bias, fox_q_norm_g,
           fox_k_norm_g, mem_kv_w, mem_q_norm_g, mem_k_norm_g, w_br_hgrn, w_br_fox, w_br_mem, w_out,
           norm_ffn_g, ffn_w_up, ffn_conv_w, ffn_conv_b, ffn_w_down):
    batch, seq, _ = x.shape
    n_mem = mem.shape[1]
    depth = w_in.shape[0]
    assert depth == 1 and seq % 512 == 0

    x2 = x.reshape(batch * seq, D_MODEL)
    for l in range(depth):
        w = w_in[l]
        c_hg, c_fox = 4 * HG_WIDTH, 3 * FOX_WIDTH
        o_ff = c_hg + c_fox
        o_mq = o_ff + FOX_HEADS
        o_gate = o_mq + MEM_WIDTH
        whg = w[:, :c_hg].astype(BF16)
        wfox = w[:, c_hg:o_ff].astype(BF16)
        wfft = jnp.pad(w[:, o_ff:o_mq].T, ((0, FOX_ROWS - FOX_HEADS), (0, 0))).astype(BF16)
        wmq = w[:, o_mq:o_gate].astype(BF16)
        wgate = w[:, o_gate:].astype(BF16)
        row = lambda v: v.reshape(1, -1).astype(F32)

        zhg, zfox, zmq, zgate, zfft = _in_proj(x2, row(norm_mix_g[l]), whg, wfox, wmq, wgate, wfft,
                                               row(mem_q_norm_g[l]), tm=512)

        ya = _hgrn(zhg, hgrn_lb_logits.astype(F32), row(hgrn_norm_g[l]), batch, seq)

        zfox3 = zfox.reshape(batch, seq, c_fox)
        fb = jnp.pad(fox_f_bias[l].astype(F32), (0, FOX_ROWS - FOX_HEADS)).reshape(FOX_ROWS, 1)
        head_id = jnp.arange(MXU_WIDTH) // FOX_DH
        bd = (head_id[:, None] == head_id[None, :]).astype(BF16) * (1.0 / FOX_DH)
        qa, ka, vt = _fox_prep(zfox3, zfft, fb,
                               jnp.tile(row(fox_q_norm_g[l]), (1, FOX_HEADS)),
                               jnp.tile(row(fox_k_norm_g[l]), (1, FOX_HEADS)), bd, batch, seq)
        qk_bound = (LOG2E * FOX_DH ** 0.5 * jnp.max(jnp.abs(fox_q_norm_g[l])) * jnp.max(jnp.abs(fox_k_norm_g[l])))
        yb = _fox_attn(qk_bound.reshape(1).astype(F32), qa, ka, vt, batch, seq, tq=512)

        mk, mv = _mem_kv(mem.reshape(batch * n_mem, D_MODEL), row(norm_mem_g[l]), mem_kv_w[l].astype(BF16),
                         row(mem_k_norm_g[l]), tm=512)
        yc = _mem_attn(zmq.reshape(batch, seq, MEM_WIDTH), mk.reshape(batch, n_mem, MEM_WIDTH),
                       mv.reshape(batch, n_mem, MEM_WIDTH), batch, seq, tq=512)

        wup = ffn_w_up[l]
        out = _merge_ffn(x2.reshape(batch, seq, D_MODEL), ya, yb, yc, zgate.reshape(batch, seq, N_BRANCH * D_MODEL),
                         w_br_hgrn[l].astype(BF16), w_br_fox[l].astype(BF16), w_br_mem[l].astype(BF16),
                         w_out[l].astype(BF16), row(norm_ffn_g[l]), wup[:, :D_FF].astype(BF16),
                         wup[:, D_FF:].astype(BF16), ffn_conv_w[l].astype(F32), row(ffn_conv_b[l]),
                         ffn_w_down[l].astype(BF16), tm=512, fc=256)
        x2 = out.reshape(batch * seq, D_MODEL)
    return x2.reshape(batch, seq, D_MODEL)
```

```python
import functools

import numpy as np
import jax
import jax.numpy as jnp
from jax import lax
from jax.experimental import pallas as pl
from jax.experimental.pallas import tpu as pltpu

F32 = jnp.float32
BF16 = jnp.bfloat16

D_MODEL = 1024
EPS = 1e-6
CHUNK = 64
HG_HEADS = 4
HG_D = 128
HG_WIDTH = HG_HEADS * HG_D
FOX_HEADS = 8
FOX_DH = 64
FOX_WIDTH = FOX_HEADS * FOX_DH
MEM_HEADS = 4
MEM_DH = 128
MEM_WIDTH = MEM_HEADS * MEM_DH
N_BRANCH = 3
D_FF = 2816
CONV_W = 3

LANES = 128
SUBLANES = 8
MXU_WIDTH = 256
VMEM_LIMIT = 56 * 1024 * 1024

NEG_BIG = -1e30
LOG2E = 1.4426950408889634


def _const_spec(shape):
    nd = len(shape)
    return pl.BlockSpec(shape, lambda *_: (0,) * nd, pipeline_mode=pl.Buffered(1))


def _params(sem):
    return pltpu.CompilerParams(dimension_semantics=sem, vmem_limit_bytes=VMEM_LIMIT)


def _rms_rows(x, g):
    return x * lax.rsqrt(jnp.mean(x * x, axis=-1, keepdims=True) + EPS) * g


def _sigmoid(x):
    return 1.0 / (1.0 + jnp.exp(-x))


def _silu(x):
    return x * _sigmoid(x)


def _inproj_kernel(x_ref, g_ref, whg_ref, wfox_ref, wmq_ref, wgate_ref, wfft_ref, mqg_ref,
                   zhg_ref, zfox_ref, zmq_ref, zgate_ref, zfft_ref):
    h = _rms_rows(x_ref[...], g_ref[...]).astype(BF16)
    for w_ref, o_ref in ((whg_ref, zhg_ref), (wfox_ref, zfox_ref), (wgate_ref, zgate_ref)):
        o_ref[...] = jnp.dot(h, w_ref[...], preferred_element_type=F32).astype(o_ref.dtype)
    zq = jnp.dot(h, wmq_ref[...], preferred_element_type=F32)
    gain = mqg_ref[...] * (MEM_DH ** -0.5 * LOG2E)
    for hd in range(MEM_HEADS):
        sl = slice(hd * MEM_DH, (hd + 1) * MEM_DH)
        zmq_ref[:, sl] = _rms_rows(zq[:, sl], gain).astype(zmq_ref.dtype)
    zfft_ref[...] = lax.dot_general(wfft_ref[...], h, (((1,), (1,)), ((), ())), preferred_element_type=F32)


def _in_proj(x2, g, whg, wfox, wmq, wgate, wfft, mqg, tm):
    n_rows = x2.shape[0]
    ws = (whg, wfox, wmq, wgate)
    row = lambda i: (i, 0)
    return pl.pallas_call(
        _inproj_kernel,
        grid=(n_rows // tm,),
        in_specs=[pl.BlockSpec((tm, D_MODEL), row), _const_spec(g.shape)]
                 + [_const_spec(w.shape) for w in ws] + [_const_spec(wfft.shape), _const_spec(mqg.shape)],
        out_specs=[pl.BlockSpec((tm, w.shape[1]), row) for w in ws]
                  + [pl.BlockSpec((wfft.shape[0], tm), lambda i: (0, i))],
        out_shape=[jax.ShapeDtypeStruct((n_rows, w.shape[1]), BF16) for w in ws]
                  + [jax.ShapeDtypeStruct((wfft.shape[0], n_rows), F32)],
        compiler_params=_params(("parallel",)),
        name="in_proj",
    )(x2, g, *ws, wfft, mqg)


HG_LEVELS = (32, 16, 8, 4, 2, 1)


def _hgrn_pivot(g, h, row):
    n, w = g.shape
    if h >= SUBLANES:
        parts = [jnp.broadcast_to(g[p:p + 1], (2 * h, w)) for p in range(h - 1, n, 2 * h)]
        return parts[0] if len(parts) == 1 else jnp.concatenate(parts, axis=0)
    if h == 1:
        return jnp.where((row & 1) != 0, pltpu.roll(g, 1, axis=0), g)
    parts = []
    row8 = lax.broadcasted_iota(jnp.int32, (SUBLANES, w), 0)
    for lo in range(0, n, SUBLANES):
        cands = [jnp.broadcast_to(g[p:p + 1], (SUBLANES, w)) for p in range(lo + h - 1, lo + SUBLANES, 2 * h)]
        piv = cands[-1]
        for j in range(len(cands) - 2, -1, -1):
            piv = jnp.where(row8 < (j + 1) * 2 * h, cands[j], piv)
        parts.append(piv)
    return jnp.concatenate(parts, axis=0)


def _hgrn_kernel(q_ref, f_ref, i_ref, go_ref, lbl_ref, ng_ref, o_ref, st_ref, *, seq):
    lbl = lbl_ref[...]
    e = jnp.exp(lbl - jnp.max(lbl, axis=0, keepdims=True))
    lb = e[0:1] / jnp.sum(e, axis=0, keepdims=True)
    ng = ng_ref[...]
    st_ref[...] = jnp.zeros_like(st_ref)

    row = lax.broadcasted_iota(jnp.int32, (CHUNK, HG_WIDTH), 0)
    row_a = lax.broadcasted_iota(jnp.int32, (CHUNK, CHUNK), 0)
    col_a = lax.broadcasted_iota(jnp.int32, (CHUNK, CHUNK), 1)
    nt = (((1,), (1,)), ((), ()))
    heads = [slice(hd * HG_D, (hd + 1) * HG_D) for hd in range(HG_HEADS)]
    n_chunks = seq // CHUNK

    def chunk_rows(c):
        return pl.ds(pl.multiple_of(c * CHUNK, CHUNK), CHUNK)

    def local_part(c):
        rows = chunk_rows(c)
        f = lb + (1.0 - lb) * _sigmoid(f_ref[rows, :].astype(F32))
        k_all = 1.0 - f
        g_all = jnp.log(f) * LOG2E
        sh = 1
        while sh < CHUNK:
            g_all = g_all + jnp.where(row >= sh, pltpu.roll(g_all, sh, axis=0), 0.0)
            sh *= 2
        q_all = _silu(q_ref[rows, :].astype(F32))
        go_all = _silu(go_ref[rows, :].astype(F32))
        a = [None] * HG_HEADS
        for h in HG_LEVELS:
            d = jnp.exp2(-jnp.abs(g_all - _hgrn_pivot(g_all, h, row)))
            t = (jnp.where((row & h) != 0, q_all, k_all) * d).astype(BF16)
            keep = ((row_a // (2 * h)) == (col_a // (2 * h))) & ((row_a & h) != 0) & ((col_a & h) == 0)
            for hd, sl in enumerate(heads):
                prod = lax.dot_general(t[:, sl], t[:, sl], nt, preferred_element_type=F32)
                a[hd] = jnp.where(keep, prod, 0.0 if a[hd] is None else a[hd])
        qk = q_all * k_all
        for hd, sl in enumerate(heads):
            a[hd] = jnp.where(row_a == col_a, jnp.sum(qk[:, sl], axis=-1, keepdims=True), a[hd]).astype(BF16)

        g_last = g_all[CHUNK - 1:CHUNK]
        q_in = (q_all * jnp.exp2(g_all)).astype(BF16)
        k_up = (k_all * jnp.exp2(g_last - g_all)).astype(BF16)
        return a, q_in, k_up, go_all, jnp.exp2(g_last)

    def state_part(c, vals):
        rows = chunk_rows(c)
        i_all = i_ref[rows, :]
        a, q_in, k_up, go_all, dec = vals
        for hd, sl in enumerate(heads):
            st = st_ref[hd]
            o = jnp.dot(jnp.concatenate([q_in[:, sl], a[hd]], axis=1),
                        jnp.concatenate([st.T.astype(BF16), i_all[:, sl]], axis=0),
                        preferred_element_type=F32)
            i_t = i_all[:, sl].astype(F32).T.astype(BF16)
            st_ref[hd] = st * dec[:, sl] + jnp.dot(i_t, k_up[:, sl], preferred_element_type=F32)
            o_ref[rows, sl] = (_rms_rows(o, ng) * go_all[:, sl]).astype(o_ref.dtype)

    def step(k, carry):
        c = 2 * k
        first = local_part(c)
        second = local_part(c + 1)
        state_part(c, first)
        state_part(c + 1, second)
        return carry

    assert n_chunks % 2 == 0
    lax.fori_loop(0, n_chunks // 2, step, 0)


def _hgrn(zhg, lb_logits, norm_g, batch, seq):
    zhg3 = zhg.reshape(batch, seq, 4 * HG_WIDTH)
    blk = lambda off: pl.BlockSpec((None, seq, HG_WIDTH), lambda b: (b, 0, off))
    return pl.pallas_call(
        functools.partial(_hgrn_kernel, seq=seq),
        grid=(batch,),
        in_specs=[blk(0), blk(1), blk(2), blk(3), _const_spec(lb_logits.shape), _const_spec(norm_g.shape)],
        out_specs=pl.BlockSpec((None, seq, HG_WIDTH), lambda b: (b, 0, 0)),
        out_shape=jax.ShapeDtypeStruct((batch, seq, HG_WIDTH), BF16),
        scratch_shapes=[pltpu.VMEM((HG_HEADS, HG_D, HG_D), F32)],
        compiler_params=_params(("parallel",)),
        name="hgrn",
    )(zhg3, zhg3, zhg3, zhg3, lb_logits, norm_g)


FOX_AUG = FOX_DH
FOX_ROWS = 16
FOX_NAUG = 6


def _split3(x):
    hi = x.astype(BF16).astype(F32)
    r = x - hi
    lo = r.astype(BF16).astype(F32)
    return hi, lo, r - lo


def _foxprep_kernel(q_ref, k_ref, v_ref, ff_ref, fb_ref, qg_ref, kg_ref, bd_ref, place_ref,
                    qa_ref, ka_ref, vt_ref, *, seq):
    for p in range(FOX_HEADS // 2):
        vt_ref[p] = v_ref[:, p * LANES:(p + 1) * LANES].astype(F32).T.astype(BF16)
    x = ff_ref[...] + fb_ref[...]
    fc = -(jnp.maximum(-x, 0.0) + jnp.log1p(jnp.exp(-jnp.abs(x))))
    tok = lax.broadcasted_iota(jnp.int32, (FOX_ROWS, seq), 1)
    sh = 1
    while sh < seq:
        fc = fc + jnp.where(tok >= sh, pltpu.roll(fc, sh, axis=1), 0.0)
        sh *= 2
    fc = fc * LOG2E

    def normed(ref, g_ref, scale):
        v = ref[...].astype(F32)
        sq = (v * v).astype(BF16)
        w = bd_ref.shape[0]
        ms = jnp.concatenate([jnp.dot(sq[:, c:c + w], bd_ref[...], preferred_element_type=F32)
                              for c in range(0, FOX_WIDTH, w)], axis=1)
        return v * lax.rsqrt(ms + EPS) * (g_ref[...] * scale)

    qn = normed(q_ref, qg_ref, FOX_DH ** -0.5 * LOG2E)
    kn = normed(k_ref, kg_ref, 1.0)
    hi, lo, lo2 = _split3(fc)
    parts = jnp.concatenate([hi, lo, lo2, jnp.ones_like(hi)], axis=0)
    aug = jnp.dot(parts.T.astype(BF16), place_ref[...],
                  preferred_element_type=F32)
    lane = lax.broadcasted_iota(jnp.int32, (seq, LANES), 1)
    for h in range(FOX_HEADS):
        aug_q = pltpu.roll(aug, (FOX_AUG - FOX_NAUG * h) % LANES, axis=1)
        aug_k = pltpu.roll(aug, (FOX_AUG - FOX_DH - FOX_NAUG * h) % LANES, axis=1) if h else aug
        pair = slice(LANES * (h // 2), LANES * (h // 2) + LANES)
        pq, pk = qn[:, pair], kn[:, pair]
        if h % 2:
            pq = pltpu.roll(pq, FOX_DH, axis=1)
            pk = pltpu.roll(pk, FOX_DH, axis=1)
        qa_ref[h] = jnp.where(lane < FOX_DH, pq, aug_q).astype(BF16)
        ka_ref[h] = jnp.where(lane < FOX_DH, pk, jnp.where(lane < FOX_AUG + FOX_NAUG, aug_k, 0.0)).astype(BF16)


def _fox_place_matrix():
    place = np.zeros((4 * FOX_ROWS, LANES), np.float32)
    for h in range(FOX_HEADS):
        for part in range(3):
            place[part * FOX_ROWS + h, FOX_NAUG * h + part] = 1.0
            place[part * FOX_ROWS + h, FOX_DH + FOX_NAUG * h + 3 + part] = -1.0
            place[3 * FOX_ROWS, FOX_NAUG * h + 3 + part] = 1.0
            place[3 * FOX_ROWS, FOX_DH + FOX_NAUG * h + part] = 1.0
    return jnp.asarray(place, BF16)


def _fox_prep(zfox3, zfft, fb, qg, kg, bd, batch, seq):
    place = _fox_place_matrix()
    head_out = pl.BlockSpec((None, FOX_HEADS, seq, LANES), lambda b: (b, 0, 0, 0))
    return pl.pallas_call(
        functools.partial(_foxprep_kernel, seq=seq),
        grid=(batch,),
        in_specs=[pl.BlockSpec((None, seq, FOX_WIDTH), lambda b: (b, 0, 0)),
                  pl.BlockSpec((None, seq, FOX_WIDTH), lambda b: (b, 0, 1)),
                  pl.BlockSpec((None, seq, FOX_WIDTH), lambda b: (b, 0, 2)),
                  pl.BlockSpec((FOX_ROWS, seq), lambda b: (0, b)),
                  _const_spec(fb.shape), _const_spec(qg.shape), _const_spec(kg.shape), _const_spec(bd.shape),
                  _const_spec(place.shape)],
        out_specs=[head_out, head_out,
                   pl.BlockSpec((None, FOX_HEADS // 2, LANES, seq), lambda b: (b, 0, 0, 0))],
        out_shape=[jax.ShapeDtypeStruct((batch, FOX_HEADS, seq, LANES), BF16)] * 2
                  + [jax.ShapeDtypeStruct((batch, FOX_HEADS // 2, LANES, seq), BF16)],
        compiler_params=_params(("parallel",)),
        name="fox_prep",
    )(zfox3, zfox3, zfox3, zfft, fb, qg, kg, bd, place)


FOX_FAST_LOGIT_BOUND = 100.0


def _fox_fast(qa_ref, ka_ref, vt_ref, o_ref, tq, n_blk):
    key = lax.broadcasted_iota(jnp.int32, (tq, tq), 0)
    qry = lax.broadcasted_iota(jnp.int32, (tq, tq), 1)
    nt = (((1,), (1,)), ((), ()))
    for i in range(n_blk):
        l = [None, None]
        acc = [None, None]
        for j in range(i + 1):
            ss = [lax.dot_general(ka_ref[hh, j * tq:(j + 1) * tq, :], qa_ref[hh, i * tq:(i + 1) * tq, :],
                                  nt, preferred_element_type=F32) for hh in range(2)]
            for hh in range(2):
                s = jnp.where(key <= qry, ss[hh], NEG_BIG) if j == i else ss[hh]
                p = jnp.exp2(s)
                vt = vt_ref[hh * FOX_DH:(hh + 1) * FOX_DH, j * tq:(j + 1) * tq]
                ps = jnp.sum(p, axis=0, keepdims=True)
                pv = jnp.dot(vt, p.astype(BF16), preferred_element_type=F32)
                l[hh] = ps if j == 0 else l[hh] + ps
                acc[hh] = pv if j == 0 else acc[hh] + pv
        out_t = jnp.concatenate([acc[hh] / l[hh] for hh in range(2)], axis=0)
        o_ref[i * tq:(i + 1) * tq, :] = out_t.T.astype(o_ref.dtype)


def _foxattn_kernel(bound_ref, qa_ref, ka_ref, vt_ref, o_ref, s_ref, m_ref, l_ref, acc_ref, *, tq, n_blk):
    fast = bound_ref[0] <= FOX_FAST_LOGIT_BOUND

    @pl.when(fast)
    def _():
        _fox_fast(qa_ref, ka_ref, vt_ref, o_ref, tq, n_blk)

    @pl.when(jnp.logical_not(fast))
    def _():
        _fox_online(qa_ref, ka_ref, vt_ref, o_ref, s_ref, m_ref, l_ref, acc_ref, tq, n_blk)


def _fox_online(qa_ref, ka_ref, vt_ref, o_ref, s_ref, m_ref, l_ref, acc_ref, tq, n_blk):
    key = lax.broadcasted_iota(jnp.int32, (tq, tq), 0)
    qry = lax.broadcasted_iota(jnp.int32, (tq, tq), 1)
    nt = (((1,), (1,)), ((), ()))
    steps = [(i, j) for i in range(n_blk) for j in range(i + 1)]

    def scores(n, hh):
        i, j = steps[n]
        s_ref[n % 2, hh] = lax.dot_general(ka_ref[hh, j * tq:(j + 1) * tq, :], qa_ref[hh, i * tq:(i + 1) * tq, :],
                                           nt, preferred_element_type=F32)

    def reduce(n, hh):
        i, j = steps[n]
        s = s_ref[n % 2, hh]
        if j == i:
            s = jnp.where(key <= qry, s, NEG_BIG)
        m_new = jnp.max(s, axis=0, keepdims=True)
        vt = vt_ref[hh * FOX_DH:(hh + 1) * FOX_DH, j * tq:(j + 1) * tq]
        if j == 0:
            p = jnp.exp2(s - m_new)
            m_ref[hh] = m_new
            l_ref[hh] = jnp.sum(p, axis=0, keepdims=True)
            acc_ref[hh] = jnp.dot(vt, p.astype(BF16), preferred_element_type=F32)
        else:
            m = m_ref[hh]
            m_new = jnp.maximum(m, m_new)
            alpha = jnp.exp2(m - m_new)
            p = jnp.exp2(s - m_new)
            m_ref[hh] = m_new
            l_ref[hh] = alpha * l_ref[hh] + jnp.sum(p, axis=0, keepdims=True)
            acc_ref[hh] = alpha * acc_ref[hh] + jnp.dot(vt, p.astype(BF16), preferred_element_type=F32)

    for hh in range(2):
        scores(0, hh)
    for n, (i, j) in enumerate(steps):
        for hh in range(2):
            if n + 1 < len(steps):
                scores(n + 1, hh)
            reduce(n, hh)
        if j == i:
            out_t = jnp.concatenate([acc_ref[hh] / l_ref[hh] for hh in range(2)], axis=0)
            o_ref[i * tq:(i + 1) * tq, :] = out_t.T.astype(o_ref.dtype)


def _fox_attn(qk_bound, qa, ka, vt, batch, seq, tq):
    pair = lambda b, p: (b, p, 0, 0)
    return pl.pallas_call(
        functools.partial(_foxattn_kernel, tq=tq, n_blk=seq // tq),
        grid=(batch, FOX_HEADS // 2),
        in_specs=[pl.BlockSpec(memory_space=pltpu.SMEM),
                  pl.BlockSpec((None, 2, seq, LANES), pair), pl.BlockSpec((None, 2, seq, LANES), pair),
                  pl.BlockSpec((None, None, LANES, seq), pair)],
        out_specs=pl.BlockSpec((None, seq, LANES), lambda b, p: (b, 0, p)),
        out_shape=jax.ShapeDtypeStruct((batch, seq, FOX_WIDTH), BF16),
        scratch_shapes=[pltpu.VMEM((2, 2, tq, tq), F32), pltpu.VMEM((2, 1, tq), F32), pltpu.VMEM((2, 1, tq), F32),
                        pltpu.VMEM((2, FOX_DH, tq), F32)],
        compiler_params=_params(("parallel", "parallel")),
        name="fox_attn",
    )(qk_bound, qa, ka, vt)


def _memkv_kernel(m_ref, g_ref, w_ref, kg_ref, k_ref, v_ref):
    h = _rms_rows(m_ref[...], g_ref[...]).astype(BF16)
    kv = jnp.dot(h, w_ref[...], preferred_element_type=F32)
    for hd in range(MEM_HEADS):
        sl = slice(hd * MEM_DH, (hd + 1) * MEM_DH)
        k_ref[:, sl] = _rms_rows(kv[:, sl], kg_ref[...]).astype(k_ref.dtype)
    v_ref[...] = kv[:, MEM_WIDTH:].astype(v_ref.dtype)


def _mem_kv(mem2, g, w, kg, tm):
    n_rows = mem2.shape[0]
    row = lambda i: (i, 0)
    return pl.pallas_call(
        _memkv_kernel,
        grid=(n_rows // tm,),
        in_specs=[pl.BlockSpec((tm, D_MODEL), row), _const_spec(g.shape), _const_spec(w.shape),
                  _const_spec(kg.shape)],
        out_specs=[pl.BlockSpec((tm, MEM_WIDTH), row)] * 2,
        out_shape=[jax.ShapeDtypeStruct((n_rows, MEM_WIDTH), BF16)] * 2,
        compiler_params=_params(("parallel",)),
        name="mem_kv",
    )(mem2, g, w, kg)


def _memattn_kernel(q_ref, k_ref, v_ref, o_ref):
    for hd in range(MEM_HEADS):
        sl = slice(hd * MEM_DH, (hd + 1) * MEM_DH)
        s = lax.dot_general(q_ref[:, sl], k_ref[:, sl], (((1,), (1,)), ((), ())), preferred_element_type=F32)
        p = jnp.exp2(s - jnp.max(s, axis=-1, keepdims=True))
        l = jnp.sum(p, axis=-1, keepdims=True)
        o = jnp.dot(p.astype(BF16), v_ref[:, sl], preferred_element_type=F32)
        o_ref[:, sl] = (o / l).astype(o_ref.dtype)


def _mem_attn(zmq3, mk3, mv3, batch, seq, tq):
    n_mem = mk3.shape[1]
    kv_spec = pl.BlockSpec((None, n_mem, MEM_WIDTH), lambda b, i: (b, 0, 0))
    return pl.pallas_call(
        _memattn_kernel,
        grid=(batch, seq // tq),
        in_specs=[pl.BlockSpec((None, tq, MEM_WIDTH), lambda b, i: (b, i, 0)), kv_spec, kv_spec],
        out_specs=pl.BlockSpec((None, tq, MEM_WIDTH), lambda b, i: (b, i, 0)),
        out_shape=jax.ShapeDtypeStruct((batch, seq, MEM_WIDTH), BF16),
        compiler_params=_params(("parallel", "parallel")),
        name="mem_attn",
    )(zmq3, mk3, mv3)


def _mergeffn_kernel(x_ref, ya_ref, yb_ref, yc_ref, gt_ref, wba_ref, wbb_ref, wbc_ref, wo_ref, ng_ref,
                     wa_ref, wv_ref, cw_ref, cb_ref, wd_ref, o_ref, carry_ref, y_ref, *, tm, fc):
    @pl.when(pl.program_id(1) == 0)
    def _():
        carry_ref[...] = jnp.zeros_like(carry_ref)

    merged = None
    for br, (yb_r, w_ref) in enumerate(((ya_ref, wba_ref), (yb_ref, wbb_ref), (yc_ref, wbc_ref))):
        gate = _sigmoid(gt_ref[:, br * D_MODEL:(br + 1) * D_MODEL].astype(F32))
        t = gate * jnp.dot(yb_r[...], w_ref[...], preferred_element_type=F32)
        merged = t if merged is None else merged + t
    x1 = x_ref[...] + jnp.dot(merged.astype(BF16), wo_ref[...], preferred_element_type=F32)
    o_ref[...] = x1

    h = _rms_rows(x1, ng_ref[...]).astype(BF16)
    row = lax.broadcasted_iota(jnp.int32, (tm, fc), 0)
    for c in range(D_FF // fc):
        sl = slice(c * fc, (c + 1) * fc)
        a = jnp.dot(h, wa_ref[:, sl], preferred_element_type=F32)
        v = jnp.dot(h, wv_ref[:, sl], preferred_element_type=F32)
        prev = carry_ref[c]
        p1 = prev[SUBLANES - 1:SUBLANES]
        p2 = prev[SUBLANES - 2:SUBLANES - 1]
        a1 = jnp.where(row == 0, p1, pltpu.roll(a, 1, axis=0))
        a2 = jnp.where(row == 0, p2, jnp.where(row == 1, p1, pltpu.roll(a, 2, axis=0)))
        carry_ref[c] = a[tm - SUBLANES:tm]
        z = a2 * cw_ref[0:1, sl] + a1 * cw_ref[1:2, sl] + a * cw_ref[2:3, sl] + cb_ref[:, sl]
        gelu = 0.5 * z * (1.0 + lax.erf(z * (2.0 ** -0.5)))
        y_ref[:, sl] = (gelu * v).astype(y_ref.dtype)
    o_ref[...] += jnp.dot(y_ref[...], wd_ref[...], preferred_element_type=F32)


def _merge_ffn(x3, ya, yb, yc, zgate3, wba, wbb, wbc, wo, ng, wa, wv, cw, cb, wd, tm, fc):
    batch, seq, _ = x3.shape
    tile = lambda w: pl.BlockSpec((None, tm, w), lambda b, t: (b, t, 0))
    consts = (wba, wbb, wbc, wo, ng, wa, wv, cw, cb, wd)
    return pl.pallas_call(
        functools.partial(_mergeffn_kernel, tm=tm, fc=fc),
        grid=(batch, seq // tm),
        in_specs=[tile(D_MODEL), tile(HG_WIDTH), tile(FOX_WIDTH), tile(MEM_WIDTH), tile(N_BRANCH * D_MODEL)]
                 + [_const_spec(c.shape) for c in consts],
        out_specs=tile(D_MODEL),
        out_shape=jax.ShapeDtypeStruct((batch, seq, D_MODEL), F32),
        scratch_shapes=[pltpu.VMEM((D_FF // fc, SUBLANES, fc), F32), pltpu.VMEM((tm, D_FF), BF16)],
        compiler_params=_params(("parallel", "arbitrary")),
        name="merge_ffn",
    )(x3, ya, yb, yc, zgate3, *consts)


def kernel(x, mem, norm_mix_g, norm_mem_g, w_in, hgrn_lb_logits, hgrn_norm_g, fox_f_bias, fox_q_norm_g,
           fox_k_norm_g, mem_kv_w, mem_q_norm_g, mem_k_norm_g, w_br_hgrn, w_br_fox, w_br_mem, w_out,
           norm_ffn_g, ffn_w_up, ffn_conv_w, ffn_conv_b, ffn_w_down):
    batch, seq, _ = x.shape
    n_mem = mem.shape[1]
    depth = w_in.shape[0]
    assert depth == 1 and seq % 512 == 0

    x2 = x.reshape(batch * seq, D_MODEL)
    for l in range(depth):
        w = w_in[l]
        c_hg, c_fox = 4 * HG_WIDTH, 3 * FOX_WIDTH
        o_ff = c_hg + c_fox
        o_mq = o_ff + FOX_HEADS
        o_gate = o_mq + MEM_WIDTH
        whg = w[:, :c_hg].astype(BF16)
        wfox = w[:, c_hg:o_ff].astype(BF16)
        wfft = jnp.pad(w[:, o_ff:o_mq].T, ((0, FOX_ROWS - FOX_HEADS), (0, 0))).astype(BF16)
        wmq = w[:, o_mq:o_gate].astype(BF16)
        wgate = w[:, o_gate:].astype(BF16)
        row = lambda v: v.reshape(1, -1).astype(F32)

        zhg, zfox, zmq, zgate, zfft = _in_proj(x2, row(norm_mix_g[l]), whg, wfox, wmq, wgate, wfft,
                                               row(mem_q_norm_g[l]), tm=512)

        ya = _hgrn(zhg, hgrn_lb_logits.astype(F32), row(hgrn_norm_g[l]), batch, seq)

        zfox3 = zfox.reshape(batch, seq, c_fox)
        fb = jnp.pad(fox_f_bias[l].astype(F32), (0, FOX_ROWS - FOX_HEADS)).reshape(FOX_ROWS, 1)
        head_id = jnp.arange(MXU_WIDTH) // FOX_DH
        bd = (head_id[:, None] == head_id[None, :]).astype(BF16) * (1.0 / FOX_DH)
        qa, ka, vt = _fox_prep(zfox3, zfft, fb,
                               jnp.tile(row(fox_q_norm_g[l]), (1, FOX_HEADS)),
                               jnp.tile(row(fox_k_norm_g[l]), (1, FOX_HEADS)), bd, batch, seq)
        qk_bound = (LOG2E * FOX_DH ** 0.5 * jnp.max(jnp.abs(fox_q_norm_g[l])) * jnp.max(jnp.abs(fox_k_norm_g[l])))
        yb = _fox_attn(qk_bound.reshape(1).astype(F32), qa, ka, vt, batch, seq, tq=512)

        mk, mv = _mem_kv(mem.reshape(batch * n_mem, D_MODEL), row(norm_mem_g[l]), mem_kv_w[l].astype(BF16),
                         row(mem_k_norm_g[l]), tm=512)
        yc = _mem_attn(zmq.reshape(batch, seq, MEM_WIDTH), mk.reshape(batch, n_mem, MEM_WIDTH),
                       mv.reshape(batch, n_mem, MEM_WIDTH), batch, seq, tq=512)

        wup = ffn_w_up[l]
        out = _merge_ffn(x2.reshape(batch, seq, D_MODEL), ya, yb, yc, zgate.reshape(batch, seq, N_BRANCH * D_MODEL),
                         w_br_hgrn[l].astype(BF16), w_br_fox[l].astype(BF16), w_br_mem[l].astype(BF16),
                         w_out[l].astype(BF16), row(norm_ffn_g[l]), wup[:, :D_FF].astype(BF16),
                         wup[:, D_FF:].astype(BF16), ffn_conv_w[l].astype(F32), row(ffn_conv_b[l]),
                         ffn_w_down[l].astype(BF16), tm=512, fc=256)
        x2 = out.reshape(batch * seq, D_MODEL)
    return x2.reshape(batch, seq, D_MODEL)
```

```python
import functools

import numpy as np
import jax
import jax.numpy as jnp
from jax import lax
from jax.experimental import pallas as pl
from jax.experimental.pallas import tpu as pltpu

F32 = jnp.float32
BF16 = jnp.bfloat16

D_MODEL = 1024
EPS = 1e-6
CHUNK = 64
HG_HEADS = 4
HG_D = 128
HG_WIDTH = HG_HEADS * HG_D
FOX_HEADS = 8
FOX_DH = 64
FOX_WIDTH = FOX_HEADS * FOX_DH
MEM_HEADS = 4
MEM_DH = 128
MEM_WIDTH = MEM_HEADS * MEM_DH
N_BRANCH = 3
D_FF = 2816
CONV_W = 3

LANES = 128
SUBLANES = 8
MXU_WIDTH = 256
VMEM_LIMIT = 56 * 1024 * 1024

NEG_BIG = -1e30
LOG2E = 1.4426950408889634


def _const_spec(shape):
    nd = len(shape)
    return pl.BlockSpec(shape, lambda *_: (0,) * nd, pipeline_mode=pl.Buffered(1))


def _params(sem):
    return pltpu.CompilerParams(dimension_semantics=sem, vmem_limit_bytes=VMEM_LIMIT)


def _rms_rows(x, g):
    return x * lax.rsqrt(jnp.mean(x * x, axis=-1, keepdims=True) + EPS) * g


def _sigmoid(x):
    return 1.0 / (1.0 + jnp.exp(-x))


def _silu(x):
    return x * _sigmoid(x)


def _inproj_kernel(x_ref, g_ref, whg_ref, wfox_ref, wmq_ref, wgate_ref, wfft_ref, mqg_ref,
                   zhg_ref, zfox_ref, zmq_ref, zgate_ref, zfft_ref):
    h = _rms_rows(x_ref[...], g_ref[...]).astype(BF16)
    for w_ref, o_ref in ((whg_ref, zhg_ref), (wfox_ref, zfox_ref), (wgate_ref, zgate_ref)):
        o_ref[...] = jnp.dot(h, w_ref[...], preferred_element_type=F32).astype(o_ref.dtype)
    zq = jnp.dot(h, wmq_ref[...], preferred_element_type=F32)
    gain = mqg_ref[...] * (MEM_DH ** -0.5 * LOG2E)
    for hd in range(MEM_HEADS):
        sl = slice(hd * MEM_DH, (hd + 1) * MEM_DH)
        zmq_ref[:, sl] = _rms_rows(zq[:, sl], gain).astype(zmq_ref.dtype)
    zfft_ref[...] = lax.dot_general(wfft_ref[...], h, (((1,), (1,)), ((), ())), preferred_element_type=F32)


def _in_proj(x2, g, whg, wfox, wmq, wgate, wfft, mqg, tm):
    n_rows = x2.shape[0]
    ws = (whg, wfox, wmq, wgate)
    row = lambda i: (i, 0)
    return pl.pallas_call(
        _inproj_kernel,
        grid=(n_rows // tm,),
        in_specs=[pl.BlockSpec((tm, D_MODEL), row), _const_spec(g.shape)]
                 + [_const_spec(w.shape) for w in ws] + [_const_spec(wfft.shape), _const_spec(mqg.shape)],
        out_specs=[pl.BlockSpec((tm, w.shape[1]), row) for w in ws]
                  + [pl.BlockSpec((wfft.shape[0], tm), lambda i: (0, i))],
        out_shape=[jax.ShapeDtypeStruct((n_rows, w.shape[1]), BF16) for w in ws]
                  + [jax.ShapeDtypeStruct((wfft.shape[0], n_rows), F32)],
        compiler_params=_params(("parallel",)),
        name="in_proj",
    )(x2, g, *ws, wfft, mqg)


HG_LEVELS = (32, 16, 8, 4, 2, 1)


def _hgrn_pivot(g, h, row):
    n, w = g.shape
    if h >= SUBLANES:
        parts = [jnp.broadcast_to(g[p:p + 1], (2 * h, w)) for p in range(h - 1, n, 2 * h)]
        return parts[0] if len(parts) == 1 else jnp.concatenate(parts, axis=0)
    if h == 1:
        return jnp.where((row & 1) != 0, pltpu.roll(g, 1, axis=0), g)
    parts = []
    row8 = lax.broadcasted_iota(jnp.int32, (SUBLANES, w), 0)
    for lo in range(0, n, SUBLANES):
        cands = [jnp.broadcast_to(g[p:p + 1], (SUBLANES, w)) for p in range(lo + h - 1, lo + SUBLANES, 2 * h)]
        piv = cands[-1]
        for j in range(len(cands) - 2, -1, -1):
            piv = jnp.where(row8 < (j + 1) * 2 * h, cands[j], piv)
        parts.append(piv)
    return jnp.concatenate(parts, axis=0)


def _hgrn_kernel(q_ref, f_ref, i_ref, go_ref, lbl_ref, ng_ref, o_ref, st_ref, *, seq):
    lbl = lbl_ref[...]
    e = jnp.exp(lbl - jnp.max(lbl, axis=0, keepdims=True))
    lb = e[0:1] / jnp.sum(e, axis=0, keepdims=True)
    ng = ng_ref[...]
    st_ref[...] = jnp.zeros_like(st_ref)

    row = lax.broadcasted_iota(jnp.int32, (CHUNK, HG_WIDTH), 0)
    row_a = lax.broadcasted_iota(jnp.int32, (CHUNK, CHUNK), 0)
    col_a = lax.broadcasted_iota(jnp.int32, (CHUNK, CHUNK), 1)
    nt = (((1,), (1,)), ((), ()))
    heads = [slice(hd * HG_D, (hd + 1) * HG_D) for hd in range(HG_HEADS)]
    n_chunks = seq // CHUNK

    def chunk_rows(c):
        return pl.ds(pl.multiple_of(c * CHUNK, CHUNK), CHUNK)

    def local_part(c):
        rows = chunk_rows(c)
        f = lb + (1.0 - lb) * _sigmoid(f_ref[rows, :].astype(F32))
        k_all = 1.0 - f
        g_all = jnp.log(f) * LOG2E
        sh = 1
        while sh < CHUNK:
            g_all = g_all + jnp.where(row >= sh, pltpu.roll(g_all, sh, axis=0), 0.0)
            sh *= 2
        q_all = _silu(q_ref[rows, :].astype(F32))
        go_all = _silu(go_ref[rows, :].astype(F32))
        a = [None] * HG_HEADS
        for h in HG_LEVELS:
            d = jnp.exp2(-jnp.abs(g_all - _hgrn_pivot(g_all, h, row)))
            t = (jnp.where((row & h) != 0, q_all, k_all) * d).astype(BF16)
            keep = ((row_a // (2 * h)) == (col_a // (2 * h))) & ((row_a & h) != 0) & ((col_a & h) == 0)
            for hd, sl in enumerate(heads):
                prod = lax.dot_general(t[:, sl], t[:, sl], nt, preferred_element_type=F32)
                a[hd] = jnp.where(keep, prod, 0.0 if a[hd] is None else a[hd])
        qk = q_all * k_all
        for hd, sl in enumerate(heads):
            a[hd] = jnp.where(row_a == col_a, jnp.sum(qk[:, sl], axis=-1, keepdims=True), a[hd]).astype(BF16)

        g_last = g_all[CHUNK - 1:CHUNK]
        q_in = (q_all * jnp.exp2(g_all)).astype(BF16)
        k_up = (k_all * jnp.exp2(g_last - g_all)).astype(BF16)
        return a, q_in, k_up, go_all, jnp.exp2(g_last)

    def state_part(c, vals):
        rows = chunk_rows(c)
        i_all = i_ref[rows, :]
        a, q_in, k_up, go_all, dec = vals
        for hd, sl in enumerate(heads):
            st = st_ref[hd]
            o = jnp.dot(jnp.concatenate([q_in[:, sl], a[hd]], axis=1),
                        jnp.concatenate([st.T.astype(BF16), i_all[:, sl]], axis=0),
                        preferred_element_type=F32)
            i_t = i_all[:, sl].astype(F32).T.astype(BF16)
            st_ref[hd] = st * dec[:, sl] + jnp.dot(i_t, k_up[:, sl], preferred_element_type=F32)
            o_ref[rows, sl] = (_rms_rows(o, ng) * go_all[:, sl]).astype(o_ref.dtype)

    def step(k, carry):
        c = 2 * k
        first = local_part(c)
        second = local_part(c + 1)
        state_part(c, first)
        state_part(c + 1, second)
        return carry

    assert n_chunks % 2 == 0
    lax.fori_loop(0, n_chunks // 2, step, 0)


def _hgrn(zhg, lb_logits, norm_g, batch, seq):
    zhg3 = zhg.reshape(batch, seq, 4 * HG_WIDTH)
    blk = lambda off: pl.BlockSpec((None, seq, HG_WIDTH), lambda b: (b, 0, off))
    return pl.pallas_call(
        functools.partial(_hgrn_kernel, seq=seq),
        grid=(batch,),
        in_specs=[blk(0), blk(1), blk(2), blk(3), _const_spec(lb_logits.shape), _const_spec(norm_g.shape)],
        out_specs=pl.BlockSpec((None, seq, HG_WIDTH), lambda b: (b, 0, 0)),
        out_shape=jax.ShapeDtypeStruct((batch, seq, HG_WIDTH), BF16),
        scratch_shapes=[pltpu.VMEM((HG_HEADS, HG_D, HG_D), F32)],
        compiler_params=_params(("parallel",)),
        name="hgrn",
    )(zhg3, zhg3, zhg3, zhg3, lb_logits, norm_g)


FOX_AUG = FOX_DH
FOX_ROWS = 16
FOX_NAUG = 6


def _split3(x):
    hi = x.astype(BF16).astype(F32)
    r = x - hi
    lo = r.astype(BF16).astype(F32)
    return hi, lo, r - lo


def _foxprep_kernel(q_ref, k_ref, v_ref, ff_ref, fb_ref, qg_ref, kg_ref, bd_ref, place_ref,
                    qa_ref, ka_ref, vt_ref, *, seq):
    for p in range(FOX_HEADS // 2):
        vt_ref[p] = v_ref[:, p * LANES:(p + 1) * LANES].astype(F32).T.astype(BF16)
    x = ff_ref[...] + fb_ref[...]
    fc = -(jnp.maximum(-x, 0.0) + jnp.log1p(jnp.exp(-jnp.abs(x))))
    tok = lax.broadcasted_iota(jnp.int32, (FOX_ROWS, seq), 1)
    sh = 1
    while sh < seq:
        fc = fc + jnp.where(tok >= sh, pltpu.roll(fc, sh, axis=1), 0.0)
        sh *= 2
    fc = fc * LOG2E

    def normed(ref, g_ref, scale):
        v = ref[...].astype(F32)
        sq = (v * v).astype(BF16)
        w = bd_ref.shape[0]
        ms = jnp.concatenate([jnp.dot(sq[:, c:c + w], bd_ref[...], preferred_element_type=F32)
                              for c in range(0, FOX_WIDTH, w)], axis=1)
        return v * lax.rsqrt(ms + EPS) * (g_ref[...] * scale)

    qn = normed(q_ref, qg_ref, FOX_DH ** -0.5 * LOG2E)
    kn = normed(k_ref, kg_ref, 1.0)
    hi, lo, lo2 = _split3(fc)
    parts = jnp.concatenate([hi, lo, lo2, jnp.ones_like(hi)], axis=0)
    aug = jnp.dot(parts.T.astype(BF16), place_ref[...],
                  preferred_element_type=F32)
    lane = lax.broadcasted_iota(jnp.int32, (seq, LANES), 1)
    for h in range(FOX_HEADS):
        aug_q = pltpu.roll(aug, (FOX_AUG - FOX_NAUG * h) % LANES, axis=1)
        aug_k = pltpu.roll(aug, (FOX_AUG - FOX_DH - FOX_NAUG * h) % LANES, axis=1) if h else aug
        pair = slice(LANES * (h // 2), LANES * (h // 2) + LANES)
        pq, pk = qn[:, pair], kn[:, pair]
        if h % 2:
            pq = pltpu.roll(pq, FOX_DH, axis=1)
            pk = pltpu.roll(pk, FOX_DH, axis=1)
        qa_ref[h] = jnp.where(lane < FOX_DH, pq, aug_q).astype(BF16)
        ka_ref[h] = jnp.where(lane < FOX_DH, pk, jnp.where(lane < FOX_AUG + FOX_NAUG, aug_k, 0.0)).astype(BF16)


def _fox_place_matrix():
    place = np.zeros((4 * FOX_ROWS, LANES), np.float32)
    for h in range(FOX_HEADS):
        for part in range(3):
            place[part * FOX_ROWS + h, FOX_NAUG * h + part] = 1.0
            place[part * FOX_ROWS + h, FOX_DH + FOX_NAUG * h + 3 + part] = -1.0
            place[3 * FOX_ROWS, FOX_NAUG * h + 3 + part] = 1.0
            place[3 * FOX_ROWS, FOX_DH + FOX_NAUG * h + part] = 1.0
    return jnp.asarray(place, BF16)


def _fox_prep(zfox3, zfft, fb, qg, kg, bd, batch, seq):
    place = _fox_place_matrix()
    head_out = pl.BlockSpec((None, FOX_HEADS, seq, LANES), lambda b: (b, 0, 0, 0))
    return pl.pallas_call(
        functools.partial(_foxprep_kernel, seq=seq),
        grid=(batch,),
        in_specs=[pl.BlockSpec((None, seq, FOX_WIDTH), lambda b: (b, 0, 0)),
                  pl.BlockSpec((None, seq, FOX_WIDTH), lambda b: (b, 0, 1)),
                  pl.BlockSpec((None, seq, FOX_WIDTH), lambda b: (b, 0, 2)),
                  pl.BlockSpec((FOX_ROWS, seq), lambda b: (0, b)),
                  _const_spec(fb.shape), _const_spec(qg.shape), _const_spec(kg.shape), _const_spec(bd.shape),
                  _const_spec(place.shape)],
        out_specs=[head_out, head_out,
                   pl.BlockSpec((None, FOX_HEADS // 2, LANES, seq), lambda b: (b, 0, 0, 0))],
        out_shape=[jax.ShapeDtypeStruct((batch, FOX_HEADS, seq, LANES), BF16)] * 2
                  + [jax.ShapeDtypeStruct((batch, FOX_HEADS // 2, LANES, seq), BF16)],
        compiler_params=_params(("parallel",)),
        name="fox_prep",
    )(zfox3, zfox3, zfox3, zfft, fb, qg, kg, bd, place)


FOX_FAST_LOGIT_BOUND = 100.0


def _fox_fast(qa_ref, ka_ref, vt_ref, o_ref, tq, n_blk):
    key = lax.broadcasted_iota(jnp.int32, (tq, tq), 0)
    qry = lax.broadcasted_iota(jnp.int32, (tq, tq), 1)
    nt = (((1,), (1,)), ((), ()))
    for i in range(n_blk):
        l = [None, None]
        acc = [None, None]
        for j in range(i + 1):
            ss = [lax.dot_general(ka_ref[hh, j * tq:(j + 1) * tq, :], qa_ref[hh, i * tq:(i + 1) * tq, :],
                                  nt, preferred_element_type=F32) for hh in range(2)]
            for hh in range(2):
                s = jnp.where(key <= qry, ss[hh], NEG_BIG) if j == i else ss[hh]
                p = jnp.exp2(s)
                vt = vt_ref[hh * FOX_DH:(hh + 1) * FOX_DH, j * tq:(j + 1) * tq]
                ps = jnp.sum(p, axis=0, keepdims=True)
                pv = jnp.dot(vt, p.astype(BF16), preferred_element_type=F32)
                l[hh] = ps if j == 0 else l[hh] + ps
                acc[hh] = pv if j == 0 else acc[hh] + pv
        out_t = jnp.concatenate([acc[hh] / l[hh] for hh in range(2)], axis=0)
        o_ref[i * tq:(i + 1) * tq, :] = out_t.T.astype(o_ref.dtype)


def _foxattn_kernel(bound_ref, qa_ref, ka_ref, vt_ref, o_ref, s_ref, m_ref, l_ref, acc_ref, *, tq, n_blk):
    fast = bound_ref[0] <= FOX_FAST_LOGIT_BOUND

    @pl.when(fast)
    def _():
        _fox_fast(qa_ref, ka_ref, vt_ref, o_ref, tq, n_blk)

    @pl.when(jnp.logical_not(fast))
    def _():
        _fox_online(qa_ref, ka_ref, vt_ref, o_ref, s_ref, m_ref, l_ref, acc_ref, tq, n_blk)


def _fox_online(qa_ref, ka_ref, vt_ref, o_ref, s_ref, m_ref, l_ref, acc_ref, tq, n_blk):
    key = lax.broadcasted_iota(jnp.int32, (tq, tq), 0)
    qry = lax.broadcasted_iota(jnp.int32, (tq, tq), 1)
    nt = (((1,), (1,)), ((), ()))
    steps = [(i, j) for i in range(n_blk) for j in range(i + 1)]

    def scores(n, hh):
        i, j = steps[n]
        s_ref[n % 2, hh] = lax.dot_general(ka_ref[hh, j * tq:(j + 1) * tq, :], qa_ref[hh, i * tq:(i + 1) * tq, :],
                                           nt, preferred_element_type=F32)

    def reduce(n, hh):
        i, j = steps[n]
        s = s_ref[n % 2, hh]
        if j == i:
            s = jnp.where(key <= qry, s, NEG_BIG)
        m_new = jnp.max(s, axis=0, keepdims=True)
        vt = vt_ref[hh * FOX_DH:(hh + 1) * FOX_DH, j * tq:(j + 1) * tq]
        if j == 0:
            p = jnp.exp2(s - m_new)
            m_ref[hh] = m_new
            l_ref[hh] = jnp.sum(p, axis=0, keepdims=True)
            acc_ref[hh] = jnp.dot(vt, p.astype(BF16), preferred_element_type=F32)
        else:
            m = m_ref[hh]
            m_new = jnp.maximum(m, m_new)
            alpha = jnp.exp2(m - m_new)
            p = jnp.exp2(s - m_new)
            m_ref[hh] = m_new
            l_ref[hh] = alpha * l_ref[hh] + jnp.sum(p, axis=0, keepdims=True)
            acc_ref[hh] = alpha * acc_ref[hh] + jnp.dot(vt, p.astype(BF16), preferred_element_type=F32)

    for hh in range(2):
        scores(0, hh)
    for n, (i, j) in enumerate(steps):
        for hh in range(2):
            if n + 1 < len(steps):
                scores(n + 1, hh)
            reduce(n, hh)
        if j == i:
            out_t = jnp.concatenate([acc_ref[hh] / l_ref[hh] for hh in range(2)], axis=0)
            o_ref[i * tq:(i + 1) * tq, :] = out_t.T.astype(o_ref.dtype)


def _fox_attn(qk_bound, qa, ka, vt, batch, seq, tq):
    pair = lambda b, p: (b, p, 0, 0)
    return pl.pallas_call(
        functools.partial(_foxattn_kernel, tq=tq, n_blk=seq // tq),
        grid=(batch, FOX_HEADS // 2),
        in_specs=[pl.BlockSpec(memory_space=pltpu.SMEM),
                  pl.BlockSpec((None, 2, seq, LANES), pair), pl.BlockSpec((None, 2, seq, LANES), pair),
                  pl.BlockSpec((None, None, LANES, seq), pair)],
        out_specs=pl.BlockSpec((None, seq, LANES), lambda b, p: (b, 0, p)),
        out_shape=jax.ShapeDtypeStruct((batch, seq, FOX_WIDTH), BF16),
        scratch_shapes=[pltpu.VMEM((2, 2, tq, tq), F32), pltpu.VMEM((2, 1, tq), F32), pltpu.VMEM((2, 1, tq), F32),
                        pltpu.VMEM((2, FOX_DH, tq), F32)],
        compiler_params=_params(("parallel", "parallel")),
        name="fox_attn",
    )(qk_bound, qa, ka, vt)


def _memkv_kernel(m_ref, g_ref, w_ref, kg_ref, k_ref, v_ref):
    h = _rms_rows(m_ref[...], g_ref[...]).astype(BF16)
    kv = jnp.dot(h, w_ref[...], preferred_element_type=F32)
    for hd in range(MEM_HEADS):
        sl = slice(hd * MEM_DH, (hd + 1) * MEM_DH)
        k_ref[:, sl] = _rms_rows(kv[:, sl], kg_ref[...]).astype(k_ref.dtype)
    v_ref[...] = kv[:, MEM_WIDTH:].astype(v_ref.dtype)


def _mem_kv(mem2, g, w, kg, tm):
    n_rows = mem2.shape[0]
    row = lambda i: (i, 0)
    return pl.pallas_call(
        _memkv_kernel,
        grid=(n_rows // tm,),
        in_specs=[pl.BlockSpec((tm, D_MODEL), row), _const_spec(g.shape), _const_spec(w.shape),
                  _const_spec(kg.shape)],
        out_specs=[pl.BlockSpec((tm, MEM_WIDTH), row)] * 2,
        out_shape=[jax.ShapeDtypeStruct((n_rows, MEM_WIDTH), BF16)] * 2,
        compiler_params=_params(("parallel",)),
        name="mem_kv",
    )(mem2, g, w, kg)


def _memattn_kernel(q_ref, k_ref, v_ref, o_ref):
    for hd in range(MEM_HEADS):
        sl = slice(hd * MEM_DH, (hd + 1) * MEM_DH)
        s = lax.dot_general(q_ref[:, sl], k_ref[:, sl], (((1,), (1,)), ((), ())), preferred_element_type=F32)
        p = jnp.exp2(s - jnp.max(s, axis=-1, keepdims=True))
        l = jnp.sum(p, axis=-1, keepdims=True)
        o = jnp.dot(p.astype(BF16), v_ref[:, sl], preferred_element_type=F32)
        o_ref[:, sl] = (o / l).astype(o_ref.dtype)


def _mem_attn(zmq3, mk3, mv3, batch, seq, tq):
    n_mem = mk3.shape[1]
    kv_spec = pl.BlockSpec((None, n_mem, MEM_WIDTH), lambda b, i: (b, 0, 0))
    return pl.pallas_call(
        _memattn_kernel,
        grid=(batch, seq // tq),
        in_specs=[pl.BlockSpec((None, tq, MEM_WIDTH), lambda b, i: (b, i, 0)), kv_spec, kv_spec],
        out_specs=pl.BlockSpec((None, tq, MEM_WIDTH), lambda b, i: (b, i, 0)),
        out_shape=jax.ShapeDtypeStruct((batch, seq, MEM_WIDTH), BF16),
        compiler_params=_params(("parallel", "parallel")),
        name="mem_attn",
    )(zmq3, mk3, mv3)


def _mergeffn_kernel(x_ref, ya_ref, yb_ref, yc_ref, gt_ref, wba_ref, wbb_ref, wbc_ref, wo_ref, ng_ref,
                     wa_ref, wv_ref, cw_ref, cb_ref, wd_ref, o_ref, carry_ref, y_ref, *, tm, fc):
    @pl.when(pl.program_id(1) == 0)
    def _():
        carry_ref[...] = jnp.zeros_like(carry_ref)

    merged = None
    for br, (yb_r, w_ref) in enumerate(((ya_ref, wba_ref), (yb_ref, wbb_ref), (yc_ref, wbc_ref))):
        gate = _sigmoid(gt_ref[:, br * D_MODEL:(br + 1) * D_MODEL].astype(F32))
        t = gate * jnp.dot(yb_r[...], w_ref[...], preferred_element_type=F32)
        merged = t if merged is None else merged + t
    x1 = x_ref[...] + jnp.dot(merged.astype(BF16), wo_ref[...], preferred_element_type=F32)
    o_ref[...] = x1

    h = _rms_rows(x1, ng_ref[...]).astype(BF16)
    row = lax.broadcasted_iota(jnp.int32, (tm, fc), 0)
    for c in range(D_FF // fc):
        sl = slice(c * fc, (c + 1) * fc)
        a = jnp.dot(h, wa_ref[:, sl], preferred_element_type=F32)
        v = jnp.dot(h, wv_ref[:, sl], preferred_element_type=F32)
        prev = carry_ref[c]
        p1 = prev[SUBLANES - 1:SUBLANES]
        p2 = prev[SUBLANES - 2:SUBLANES - 1]
        a1 = jnp.where(row == 0, p1, pltpu.roll(a, 1, axis=0))
        a2 = jnp.where(row == 0, p2, jnp.where(row == 1, p1, pltpu.roll(a, 2, axis=0)))
        carry_ref[c] = a[tm - SUBLANES:tm]
        z = a2 * cw_ref[0:1, sl] + a1 * cw_ref[1:2, sl] + a * cw_ref[2:3, sl] + cb_ref[:, sl]
        gelu = 0.5 * z * (1.0 + lax.erf(z * (2.0 ** -0.5)))
        y_ref[:, sl] = (gelu * v).astype(y_ref.dtype)
    o_ref[...] += jnp.dot(y_ref[...], wd_ref[...], preferred_element_type=F32)


def _merge_ffn(x3, ya, yb, yc, zgate3, wba, wbb, wbc, wo, ng, wa, wv, cw, cb, wd, tm, fc):
    batch, seq, _ = x3.shape
    tile = lambda w: pl.BlockSpec((None, tm, w), lambda b, t: (b, t, 0))
    consts = (wba, wbb, wbc, wo, ng, wa, wv, cw, cb, wd)
    return pl.pallas_call(
        functools.partial(_mergeffn_kernel, tm=tm, fc=fc),
        grid=(batch, seq // tm),
        in_specs=[tile(D_MODEL), tile(HG_WIDTH), tile(FOX_WIDTH), tile(MEM_WIDTH), tile(N_BRANCH * D_MODEL)]
                 + [_const_spec(c.shape) for c in consts],
        out_specs=tile(D_MODEL),
        out_shape=jax.ShapeDtypeStruct((batch, seq, D_MODEL), F32),
        scratch_shapes=[pltpu.VMEM((D_FF // fc, SUBLANES, fc), F32), pltpu.VMEM((tm, D_FF), BF16)],
        compiler_params=_params(("parallel", "arbitrary")),
        name="merge_ffn",
    )(x3, ya, yb, yc, zgate3, *consts)


def kernel(x, mem, norm_mix_g, norm_mem_g, w_in, hgrn_lb_logits, hgrn_norm_g, fox_f_bias, fox_q_norm_g,
           fox_k_norm_g, mem_kv_w, mem_q_norm_g, mem_k_norm_g, w_br_hgrn, w_br_fox, w_br_mem, w_out,
           norm_ffn_g, ffn_w_up, ffn_conv_w, ffn_conv_b, ffn_w_down):
    batch, seq, _ = x.shape
    n_mem = mem.shape[1]
    depth = w_in.shape[0]
    assert depth == 1 and seq % 512 == 0

    x2 = x.reshape(batch * seq, D_MODEL)
    for l in range(depth):
        w = w_in[l]
        c_hg, c_fox = 4 * HG_WIDTH, 3 * FOX_WIDTH
        o_ff = c_hg + c_fox
        o_mq = o_ff + FOX_HEADS
        o_gate = o_mq + MEM_WIDTH
        whg = w[:, :c_hg].astype(BF16)
        wfox = w[:, c_hg:o_ff].astype(BF16)
        wfft = jnp.pad(w[:, o_ff:o_mq].T, ((0, FOX_ROWS - FOX_HEADS), (0, 0))).astype(BF16)
        wmq = w[:, o_mq:o_gate].astype(BF16)
        wgate = w[:, o_gate:].astype(BF16)
        row = lambda v: v.reshape(1, -1).astype(F32)

        zhg, zfox, zmq, zgate, zfft = _in_proj(x2, row(norm_mix_g[l]), whg, wfox, wmq, wgate, wfft,
                                               row(mem_q_norm_g[l]), tm=1024)

        ya = _hgrn(zhg, hgrn_lb_logits.astype(F32), row(hgrn_norm_g[l]), batch, seq)

        zfox3 = zfox.reshape(batch, seq, c_fox)
        fb = jnp.pad(fox_f_bias[l].astype(F32), (0, FOX_ROWS - FOX_HEADS)).reshape(FOX_ROWS, 1)
        head_id = jnp.arange(MXU_WIDTH) // FOX_DH
        bd = (head_id[:, None] == head_id[None, :]).astype(BF16) * (1.0 / FOX_DH)
        qa, ka, vt = _fox_prep(zfox3, zfft, fb,
                               jnp.tile(row(fox_q_norm_g[l]), (1, FOX_HEADS)),
                               jnp.tile(row(fox_k_norm_g[l]), (1, FOX_HEADS)), bd, batch, seq)
        qk_bound = (LOG2E * FOX_DH ** 0.5 * jnp.max(jnp.abs(fox_q_norm_g[l])) * jnp.max(jnp.abs(fox_k_norm_g[l])))
        yb = _fox_attn(qk_bound.reshape(1).astype(F32), qa, ka, vt, batch, seq, tq=512)

        mk, mv = _mem_kv(mem.reshape(batch * n_mem, D_MODEL), row(norm_mem_g[l]), mem_kv_w[l].astype(BF16),
                         row(mem_k_norm_g[l]), tm=512)
        yc = _mem_attn(zmq.reshape(batch, seq, MEM_WIDTH), mk.reshape(batch, n_mem, MEM_WIDTH),
                       mv.reshape(batch, n_mem, MEM_WIDTH), batch, seq, tq=512)

        wup = ffn_w_up[l]
        out = _merge_ffn(x2.reshape(batch, seq, D_MODEL), ya, yb, yc, zgate.reshape(batch, seq, N_BRANCH * D_MODEL),
                         w_br_hgrn[l].astype(BF16), w_br_fox[l].astype(BF16), w_br_mem[l].astype(BF16),
                         w_out[l].astype(BF16), row(norm_ffn_g[l]), wup[:, :D_FF].astype(BF16),
                         wup[:, D_FF:].astype(BF16), ffn_conv_w[l].astype(F32), row(ffn_conv_b[l]),
                         ffn_w_down[l].astype(BF16), tm=512, fc=256)
        x2 = out.reshape(batch * seq, D_MODEL)
    return x2.reshape(batch, seq, D_MODEL)
```

```python
import functools

import numpy as np
import jax
import jax.numpy as jnp
from jax import lax
from jax.experimental import pallas as pl
from jax.experimental.pallas import tpu as pltpu

F32 = jnp.float32
BF16 = jnp.bfloat16

D_MODEL = 1024
EPS = 1e-6
CHUNK = 64
HG_HEADS = 4
HG_D = 128
HG_WIDTH = HG_HEADS * HG_D
FOX_HEADS = 8
FOX_DH = 64
FOX_WIDTH = FOX_HEADS * FOX_DH
MEM_HEADS = 4
MEM_DH = 128
MEM_WIDTH = MEM_HEADS * MEM_DH
N_BRANCH = 3
D_FF = 2816
CONV_W = 3

LANES = 128
SUBLANES = 8
MXU_WIDTH = 256
VMEM_LIMIT = 56 * 1024 * 1024

NEG_BIG = -1e30
LOG2E = 1.4426950408889634


def _const_spec(shape):
    nd = len(shape)
    return pl.BlockSpec(shape, lambda *_: (0,) * nd, pipeline_mode=pl.Buffered(1))


def _params(sem):
    return pltpu.CompilerParams(dimension_semantics=sem, vmem_limit_bytes=VMEM_LIMIT)


def _rms_rows(x, g):
    return x * lax.rsqrt(jnp.mean(x * x, axis=-1, keepdims=True) + EPS) * g


def _sigmoid(x):
    return 1.0 / (1.0 + jnp.exp(-x))


def _silu(x):
    return x * _sigmoid(x)


def _inproj_kernel(x_ref, g_ref, whg_ref, wfox_ref, wmq_ref, wgate_ref, wfft_ref, mqg_ref,
                   zhg_ref, zfox_ref, zmq_ref, zgate_ref, zfft_ref):
    h = _rms_rows(x_ref[...], g_ref[...]).astype(BF16)
    for w_ref, o_ref in ((whg_ref, zhg_ref), (wfox_ref, zfox_ref), (wgate_ref, zgate_ref)):
        o_ref[...] = jnp.dot(h, w_ref[...], preferred_element_type=F32).astype(o_ref.dtype)
    zq = jnp.dot(h, wmq_ref[...], preferred_element_type=F32)
    gain = mqg_ref[...] * (MEM_DH ** -0.5 * LOG2E)
    for hd in range(MEM_HEADS):
        sl = slice(hd * MEM_DH, (hd + 1) * MEM_DH)
        zmq_ref[:, sl] = _rms_rows(zq[:, sl], gain).astype(zmq_ref.dtype)
    zfft_ref[...] = lax.dot_general(wfft_ref[...], h, (((1,), (1,)), ((), ())), preferred_element_type=F32)


def _in_proj(x2, g, whg, wfox, wmq, wgate, wfft, mqg, tm):
    n_rows = x2.shape[0]
    ws = (whg, wfox, wmq, wgate)
    row = lambda i: (i, 0)
    return pl.pallas_call(
        _inproj_kernel,
        grid=(n_rows // tm,),
        in_specs=[pl.BlockSpec((tm, D_MODEL), row), _const_spec(g.shape)]
                 + [_const_spec(w.shape) for w in ws] + [_const_spec(wfft.shape), _const_spec(mqg.shape)],
        out_specs=[pl.BlockSpec((tm, w.shape[1]), row) for w in ws]
                  + [pl.BlockSpec((wfft.shape[0], tm), lambda i: (0, i))],
        out_shape=[jax.ShapeDtypeStruct((n_rows, w.shape[1]), BF16) for w in ws]
                  + [jax.ShapeDtypeStruct((wfft.shape[0], n_rows), F32)],
        compiler_params=_params(("parallel",)),
        name="in_proj",
    )(x2, g, *ws, wfft, mqg)


HG_LEVELS = (32, 16, 8, 4, 2, 1)


def _hgrn_pivot(g, h, row):
    n, w = g.shape
    if h >= SUBLANES:
        parts = [jnp.broadcast_to(g[p:p + 1], (2 * h, w)) for p in range(h - 1, n, 2 * h)]
        return parts[0] if len(parts) == 1 else jnp.concatenate(parts, axis=0)
    if h == 1:
        return jnp.where((row & 1) != 0, pltpu.roll(g, 1, axis=0), g)
    parts = []
    row8 = lax.broadcasted_iota(jnp.int32, (SUBLANES, w), 0)
    for lo in range(0, n, SUBLANES):
        cands = [jnp.broadcast_to(g[p:p + 1], (SUBLANES, w)) for p in range(lo + h - 1, lo + SUBLANES, 2 * h)]
        piv = cands[-1]
        for j in range(len(cands) - 2, -1, -1):
            piv = jnp.where(row8 < (j + 1) * 2 * h, cands[j], piv)
        parts.append(piv)
    return jnp.concatenate(parts, axis=0)


def _hgrn_kernel(q_ref, f_ref, i_ref, go_ref, lbl_ref, ng_ref, o_ref, st_ref, *, seq):
    lbl = lbl_ref[...]
    e = jnp.exp(lbl - jnp.max(lbl, axis=0, keepdims=True))
    lb = e[0:1] / jnp.sum(e, axis=0, keepdims=True)
    ng = ng_ref[...]
    st_ref[...] = jnp.zeros_like(st_ref)

    row = lax.broadcasted_iota(jnp.int32, (CHUNK, HG_WIDTH), 0)
    row_a = lax.broadcasted_iota(jnp.int32, (CHUNK, CHUNK), 0)
    col_a = lax.broadcasted_iota(jnp.int32, (CHUNK, CHUNK), 1)
    nt = (((1,), (1,)), ((), ()))
    heads = [slice(hd * HG_D, (hd + 1) * HG_D) for hd in range(HG_HEADS)]
    n_chunks = seq // CHUNK

    def chunk_rows(c):
        return pl.ds(pl.multiple_of(c * CHUNK, CHUNK), CHUNK)

    def local_part(c):
        rows = chunk_rows(c)
        f = lb + (1.0 - lb) * _sigmoid(f_ref[rows, :].astype(F32))
        k_all = 1.0 - f
        g_all = jnp.log(f) * LOG2E
        sh = 1
        while sh < CHUNK:
            g_all = g_all + jnp.where(row >= sh, pltpu.roll(g_all, sh, axis=0), 0.0)
            sh *= 2
        q_all = _silu(q_ref[rows, :].astype(F32))
        go_all = _silu(go_ref[rows, :].astype(F32))
        a = [None] * HG_HEADS
        for h in HG_LEVELS:
            d = jnp.exp2(-jnp.abs(g_all - _hgrn_pivot(g_all, h, row)))
            t = (jnp.where((row & h) != 0, q_all, k_all) * d).astype(BF16)
            keep = ((row_a // (2 * h)) == (col_a // (2 * h))) & ((row_a & h) != 0) & ((col_a & h) == 0)
            for hd, sl in enumerate(heads):
                prod = lax.dot_general(t[:, sl], t[:, sl], nt, preferred_element_type=F32)
                a[hd] = jnp.where(keep, prod, 0.0 if a[hd] is None else a[hd])
        qk = q_all * k_all
        for hd, sl in enumerate(heads):
            a[hd] = jnp.where(row_a == col_a, jnp.sum(qk[:, sl], axis=-1, keepdims=True), a[hd]).astype(BF16)

        g_last = g_all[CHUNK - 1:CHUNK]
        q_in = (q_all * jnp.exp2(g_all)).astype(BF16)
        k_up = (k_all * jnp.exp2(g_last - g_all)).astype(BF16)
        return a, q_in, k_up, go_all, jnp.exp2(g_last)

    def state_part(c, vals):
        rows = chunk_rows(c)
        i_all = i_ref[rows, :]
        a, q_in, k_up, go_all, dec = vals
        for hd, sl in enumerate(heads):
            st = st_ref[hd]
            o = jnp.dot(jnp.concatenate([q_in[:, sl], a[hd]], axis=1),
                        jnp.concatenate([st.T.astype(BF16), i_all[:, sl]], axis=0),
                        preferred_element_type=F32)
            i_t = i_all[:, sl].astype(F32).T.astype(BF16)
            st_ref[hd] = st * dec[:, sl] + jnp.dot(i_t, k_up[:, sl], preferred_element_type=F32)
            o_ref[rows, sl] = (_rms_rows(o, ng) * go_all[:, sl]).astype(o_ref.dtype)

    def step(k, carry):
        c = 2 * k
        first = local_part(c)
        second = local_part(c + 1)
        state_part(c, first)
        state_part(c + 1, second)
        return carry

    assert n_chunks % 2 == 0
    lax.fori_loop(0, n_chunks // 2, step, 0)


def _hgrn(zhg, lb_logits, norm_g, batch, seq):
    zhg3 = zhg.reshape(batch, seq, 4 * HG_WIDTH)
    blk = lambda off: pl.BlockSpec((None, seq, HG_WIDTH), lambda b: (b, 0, off))
    return pl.pallas_call(
        functools.partial(_hgrn_kernel, seq=seq),
        grid=(batch,),
        in_specs=[blk(0), blk(1), blk(2), blk(3), _const_spec(lb_logits.shape), _const_spec(norm_g.shape)],
        out_specs=pl.BlockSpec((None, seq, HG_WIDTH), lambda b: (b, 0, 0)),
        out_shape=jax.ShapeDtypeStruct((batch, seq, HG_WIDTH), BF16),
        scratch_shapes=[pltpu.VMEM((HG_HEADS, HG_D, HG_D), F32)],
        compiler_params=_params(("parallel",)),
        name="hgrn",
    )(zhg3, zhg3, zhg3, zhg3, lb_logits, norm_g)


FOX_AUG = FOX_DH
FOX_ROWS = 16
FOX_NAUG = 6


def _split3(x):
    hi = x.astype(BF16).astype(F32)
    r = x - hi
    lo = r.astype(BF16).astype(F32)
    return hi, lo, r - lo


def _foxprep_kernel(q_ref, k_ref, v_ref, ff_ref, fb_ref, qg_ref, kg_ref, bd_ref, place_ref,
                    qa_ref, ka_ref, vt_ref, *, seq):
    for p in range(FOX_HEADS // 2):
        vt_ref[p] = v_ref[:, p * LANES:(p + 1) * LANES].astype(F32).T.astype(BF16)
    x = ff_ref[...] + fb_ref[...]
    fc = -(jnp.maximum(-x, 0.0) + jnp.log1p(jnp.exp(-jnp.abs(x))))
    tok = lax.broadcasted_iota(jnp.int32, (FOX_ROWS, seq), 1)
    sh = 1
    while sh < seq:
        fc = fc + jnp.where(tok >= sh, pltpu.roll(fc, sh, axis=1), 0.0)
        sh *= 2
    fc = fc * LOG2E

    def normed(ref, g_ref, scale):
        v = ref[...].astype(F32)
        sq = (v * v).astype(BF16)
        w = bd_ref.shape[0]
        ms = jnp.concatenate([jnp.dot(sq[:, c:c + w], bd_ref[...], preferred_element_type=F32)
                              for c in range(0, FOX_WIDTH, w)], axis=1)
        return v * lax.rsqrt(ms + EPS) * (g_ref[...] * scale)

    qn = normed(q_ref, qg_ref, FOX_DH ** -0.5 * LOG2E)
    kn = normed(k_ref, kg_ref, 1.0)
    hi, lo, lo2 = _split3(fc)
    parts = jnp.concatenate([hi, lo, lo2, jnp.ones_like(hi)], axis=0)
    aug = jnp.dot(parts.T.astype(BF16), place_ref[...],
                  preferred_element_type=F32)
    lane = lax.broadcasted_iota(jnp.int32, (seq, LANES), 1)
    for h in range(FOX_HEADS):
        aug_q = pltpu.roll(aug, (FOX_AUG - FOX_NAUG * h) % LANES, axis=1)
        aug_k = pltpu.roll(aug, (FOX_AUG - FOX_DH - FOX_NAUG * h) % LANES, axis=1) if h else aug
        pair = slice(LANES * (h // 2), LANES * (h // 2) + LANES)
        pq, pk = qn[:, pair], kn[:, pair]
        if h % 2:
            pq = pltpu.roll(pq, FOX_DH, axis=1)
            pk = pltpu.roll(pk, FOX_DH, axis=1)
        qa_ref[h] = jnp.where(lane < FOX_DH, pq, aug_q).astype(BF16)
        ka_ref[h] = jnp.where(lane < FOX_DH, pk, jnp.where(lane < FOX_AUG + FOX_NAUG, aug_k, 0.0)).astype(BF16)


def _fox_place_matrix():
    place = np.zeros((4 * FOX_ROWS, LANES), np.float32)
    for h in range(FOX_HEADS):
        for part in range(3):
            place[part * FOX_ROWS + h, FOX_NAUG * h + part] = 1.0
            place[part * FOX_ROWS + h, FOX_DH + FOX_NAUG * h + 3 + part] = -1.0
            place[3 * FOX_ROWS, FOX_NAUG * h + 3 + part] = 1.0
            place[3 * FOX_ROWS, FOX_DH + FOX_NAUG * h + part] = 1.0
    return jnp.asarray(place, BF16)


def _fox_prep(zfox3, zfft, fb, qg, kg, bd, batch, seq):
    place = _fox_place_matrix()
    head_out = pl.BlockSpec((None, FOX_HEADS, seq, LANES), lambda b: (b, 0, 0, 0))
    return pl.pallas_call(
        functools.partial(_foxprep_kernel, seq=seq),
        grid=(batch,),
        in_specs=[pl.BlockSpec((None, seq, FOX_WIDTH), lambda b: (b, 0, 0)),
                  pl.BlockSpec((None, seq, FOX_WIDTH), lambda b: (b, 0, 1)),
                  pl.BlockSpec((None, seq, FOX_WIDTH), lambda b: (b, 0, 2)),
                  pl.BlockSpec((FOX_ROWS, seq), lambda b: (0, b)),
                  _const_spec(fb.shape), _const_spec(qg.shape), _const_spec(kg.shape), _const_spec(bd.shape),
                  _const_spec(place.shape)],
        out_specs=[head_out, head_out,
                   pl.BlockSpec((None, FOX_HEADS // 2, LANES, seq), lambda b: (b, 0, 0, 0))],
        out_shape=[jax.ShapeDtypeStruct((batch, FOX_HEADS, seq, LANES), BF16)] * 2
                  + [jax.ShapeDtypeStruct((batch, FOX_HEADS // 2, LANES, seq), BF16)],
        compiler_params=_params(("parallel",)),
        name="fox_prep",
    )(zfox3, zfox3, zfox3, zfft, fb, qg, kg, bd, place)


FOX_FAST_LOGIT_BOUND = 100.0


def _fox_fast(qa_ref, ka_ref, vt_ref, o_ref, tq, n_blk):
    key = lax.broadcasted_iota(jnp.int32, (tq, tq), 0)
    qry = lax.broadcasted_iota(jnp.int32, (tq, tq), 1)
    nt = (((1,), (1,)), ((), ()))
    for i in range(n_blk):
        l = [None, None]
        acc = [None, None]
        for j in range(i + 1):
            ss = [lax.dot_general(ka_ref[hh, j * tq:(j + 1) * tq, :], qa_ref[hh, i * tq:(i + 1) * tq, :],
                                  nt, preferred_element_type=F32) for hh in range(2)]
            for hh in range(2):
                s = jnp.where(key <= qry, ss[hh], NEG_BIG) if j == i else ss[hh]
                p = jnp.exp2(s)
                vt = vt_ref[hh * FOX_DH:(hh + 1) * FOX_DH, j * tq:(j + 1) * tq]
                ps = jnp.sum(p, axis=0, keepdims=True)
                pv = jnp.dot(vt, p.astype(BF16), preferred_element_type=F32)
                l[hh] = ps if j == 0 else l[hh] + ps
                acc[hh] = pv if j == 0 else acc[hh] + pv
        out_t = jnp.concatenate([acc[hh] / l[hh] for hh in range(2)], axis=0)
        o_ref[i * tq:(i + 1) * tq, :] = out_t.T.astype(o_ref.dtype)


def _foxattn_kernel(bound_ref, qa_ref, ka_ref, vt_ref, o_ref, s_ref, m_ref, l_ref, acc_ref, *, tq, n_blk):
    fast = bound_ref[0] <= FOX_FAST_LOGIT_BOUND

    @pl.when(fast)
    def _():
        _fox_fast(qa_ref, ka_ref, vt_ref, o_ref, tq, n_blk)

    @pl.when(jnp.logical_not(fast))
    def _():
        _fox_online(qa_ref, ka_ref, vt_ref, o_ref, s_ref, m_ref, l_ref, acc_ref, tq, n_blk)


def _fox_online(qa_ref, ka_ref, vt_ref, o_ref, s_ref, m_ref, l_ref, acc_ref, tq, n_blk):
    key = lax.broadcasted_iota(jnp.int32, (tq, tq), 0)
    qry = lax.broadcasted_iota(jnp.int32, (tq, tq), 1)
    nt = (((1,), (1,)), ((), ()))
    steps = [(i, j) for i in range(n_blk) for j in range(i + 1)]

    def scores(n, hh):
        i, j = steps[n]
        s_ref[n % 2, hh] = lax.dot_general(ka_ref[hh, j * tq:(j + 1) * tq, :], qa_ref[hh, i * tq:(i + 1) * tq, :],
                                           nt, preferred_element_type=F32)

    def reduce(n, hh):
        i, j = steps[n]
        s = s_ref[n % 2, hh]
        if j == i:
            s = jnp.where(key <= qry, s, NEG_BIG)
        m_new = jnp.max(s, axis=0, keepdims=True)
        vt = vt_ref[hh * FOX_DH:(hh + 1) * FOX_DH, j * tq:(j + 1) * tq]
        if j == 0:
            p = jnp.exp2(s - m_new)
            m_ref[hh] = m_new
            l_ref[hh] = jnp.sum(p, axis=0, keepdims=True)
            acc_ref[hh] = jnp.dot(vt, p.astype(BF16), preferred_element_type=F32)
        else:
            m = m_ref[hh]
            m_new = jnp.maximum(m, m_new)
            alpha = jnp.exp2(m - m_new)
            p = jnp.exp2(s - m_new)
            m_ref[hh] = m_new
            l_ref[hh] = alpha * l_ref[hh] + jnp.sum(p, axis=0, keepdims=True)
            acc_ref[hh] = alpha * acc_ref[hh] + jnp.dot(vt, p.astype(BF16), preferred_element_type=F32)

    for hh in range(2):
        scores(0, hh)
    for n, (i, j) in enumerate(steps):
        for hh in range(2):
            if n + 1 < len(steps):
                scores(n + 1, hh)
            reduce(n, hh)
        if j == i:
            out_t = jnp.concatenate([acc_ref[hh] / l_ref[hh] for hh in range(2)], axis=0)
            o_ref[i * tq:(i + 1) * tq, :] = out_t.T.astype(o_ref.dtype)


def _fox_attn(qk_bound, qa, ka, vt, batch, seq, tq):
    pair = lambda b, p: (b, p, 0, 0)
    return pl.pallas_call(
        functools.partial(_foxattn_kernel, tq=tq, n_blk=seq // tq),
        grid=(batch, FOX_HEADS // 2),
        in_specs=[pl.BlockSpec(memory_space=pltpu.SMEM),
                  pl.BlockSpec((None, 2, seq, LANES), pair), pl.BlockSpec((None, 2, seq, LANES), pair),
                  pl.BlockSpec((None, None, LANES, seq), pair)],
        out_specs=pl.BlockSpec((None, seq, LANES), lambda b, p: (b, 0, p)),
        out_shape=jax.ShapeDtypeStruct((batch, seq, FOX_WIDTH), BF16),
        scratch_shapes=[pltpu.VMEM((2, 2, tq, tq), F32), pltpu.VMEM((2, 1, tq), F32), pltpu.VMEM((2, 1, tq), F32),
                        pltpu.VMEM((2, FOX_DH, tq), F32)],
        compiler_params=_params(("parallel", "parallel")),
        name="fox_attn",
    )(qk_bound, qa, ka, vt)


def _memkv_kernel(m_ref, g_ref, w_ref, kg_ref, k_ref, v_ref):
    h = _rms_rows(m_ref[...], g_ref[...]).astype(BF16)
    kv = jnp.dot(h, w_ref[...], preferred_element_type=F32)
    for hd in range(MEM_HEADS):
        sl = slice(hd * MEM_DH, (hd + 1) * MEM_DH)
        k_ref[:, sl] = _rms_rows(kv[:, sl], kg_ref[...]).astype(k_ref.dtype)
    v_ref[...] = kv[:, MEM_WIDTH:].astype(v_ref.dtype)


def _mem_kv(mem2, g, w, kg, tm):
    n_rows = mem2.shape[0]
    row = lambda i: (i, 0)
    return pl.pallas_call(
        _memkv_kernel,
        grid=(n_rows // tm,),
        in_specs=[pl.BlockSpec((tm, D_MODEL), row), _const_spec(g.shape), _const_spec(w.shape),
                  _const_spec(kg.shape)],
        out_specs=[pl.BlockSpec((tm, MEM_WIDTH), row)] * 2,
        out_shape=[jax.ShapeDtypeStruct((n_rows, MEM_WIDTH), BF16)] * 2,
        compiler_params=_params(("parallel",)),
        name="mem_kv",
    )(mem2, g, w, kg)


def _memattn_kernel(q_ref, k_ref, v_ref, o_ref):
    for hd in range(MEM_HEADS):
        sl = slice(hd * MEM_DH, (hd + 1) * MEM_DH)
        s = lax.dot_general(q_ref[:, sl], k_ref[:, sl], (((1,), (1,)), ((), ())), preferred_element_type=F32)
        p = jnp.exp2(s - jnp.max(s, axis=-1, keepdims=True))
        l = jnp.sum(p, axis=-1, keepdims=True)
        o = jnp.dot(p.astype(BF16), v_ref[:, sl], preferred_element_type=F32)
        o_ref[:, sl] = (o / l).astype(o_ref.dtype)


def _mergeffn_kernel(x_ref, ya_ref, yb_ref, qm_ref, mk_ref, mv_ref, gt_ref, wba_ref, wbb_ref, wbc_ref, wo_ref,
                     ng_ref, wa_ref, wv_ref, cw_ref, cb_ref, wd_ref, o_ref, carry_ref, y_ref, yc_ref, *, tm, fc):
    @pl.when(pl.program_id(1) == 0)
    def _():
        carry_ref[...] = jnp.zeros_like(carry_ref)

    _memattn_kernel(qm_ref, mk_ref, mv_ref, yc_ref)

    merged = None
    for br, (yb_r, w_ref) in enumerate(((ya_ref, wba_ref), (yb_ref, wbb_ref), (yc_ref, wbc_ref))):
        gate = _sigmoid(gt_ref[:, br * D_MODEL:(br + 1) * D_MODEL].astype(F32))
        t = gate * jnp.dot(yb_r[...], w_ref[...], preferred_element_type=F32)
        merged = t if merged is None else merged + t
    x1 = x_ref[...] + jnp.dot(merged.astype(BF16), wo_ref[...], preferred_element_type=F32)
    o_ref[...] = x1

    h = _rms_rows(x1, ng_ref[...]).astype(BF16)
    row = lax.broadcasted_iota(jnp.int32, (tm, fc), 0)
    for c in range(D_FF // fc):
        sl = slice(c * fc, (c + 1) * fc)
        a = jnp.dot(h, wa_ref[:, sl], preferred_element_type=F32)
        v = jnp.dot(h, wv_ref[:, sl], preferred_element_type=F32)
        prev = carry_ref[c]
        p1 = prev[SUBLANES - 1:SUBLANES]
        p2 = prev[SUBLANES - 2:SUBLANES - 1]
        a1 = jnp.where(row == 0, p1, pltpu.roll(a, 1, axis=0))
        a2 = jnp.where(row == 0, p2, jnp.where(row == 1, p1, pltpu.roll(a, 2, axis=0)))
        carry_ref[c] = a[tm - SUBLANES:tm]
        z = a2 * cw_ref[0:1, sl] + a1 * cw_ref[1:2, sl] + a * cw_ref[2:3, sl] + cb_ref[:, sl]
        gelu = 0.5 * z * (1.0 + lax.erf(z * (2.0 ** -0.5)))
        y_ref[:, sl] = (gelu * v).astype(y_ref.dtype)
    o_ref[...] += jnp.dot(y_ref[...], wd_ref[...], preferred_element_type=F32)


def _merge_ffn(x3, ya, yb, zmq3, mk3, mv3, zgate3, wba, wbb, wbc, wo, ng, wa, wv, cw, cb, wd, tm, fc):
    batch, seq, _ = x3.shape
    n_mem = mk3.shape[1]
    tile = lambda w: pl.BlockSpec((None, tm, w), lambda b, t: (b, t, 0))
    kv_spec = pl.BlockSpec((None, n_mem, MEM_WIDTH), lambda b, t: (b, 0, 0))
    consts = (wba, wbb, wbc, wo, ng, wa, wv, cw, cb, wd)
    return pl.pallas_call(
        functools.partial(_mergeffn_kernel, tm=tm, fc=fc),
        grid=(batch, seq // tm),
        in_specs=[tile(D_MODEL), tile(HG_WIDTH), tile(FOX_WIDTH), tile(MEM_WIDTH), kv_spec, kv_spec,
                  tile(N_BRANCH * D_MODEL)] + [_const_spec(c.shape) for c in consts],
        out_specs=tile(D_MODEL),
        out_shape=jax.ShapeDtypeStruct((batch, seq, D_MODEL), F32),
        scratch_shapes=[pltpu.VMEM((D_FF // fc, SUBLANES, fc), F32), pltpu.VMEM((tm, D_FF), BF16),
                        pltpu.VMEM((tm, MEM_WIDTH), BF16)],
        compiler_params=_params(("parallel", "arbitrary")),
        name="merge_ffn",
    )(x3, ya, yb, zmq3, mk3, mv3, zgate3, *consts)


def kernel(x, mem, norm_mix_g, norm_mem_g, w_in, hgrn_lb_logits, hgrn_norm_g, fox_f_bias, fox_q_norm_g,
           fox_k_norm_g, mem_kv_w, mem_q_norm_g, mem_k_norm_g, w_br_hgrn, w_br_fox, w_br_mem, w_out,
           norm_ffn_g, ffn_w_up, ffn_conv_w, ffn_conv_b, ffn_w_down):
    batch, seq, _ = x.shape
    n_mem = mem.shape[1]
    depth = w_in.shape[0]
    assert depth == 1 and seq % 512 == 0

    x2 = x.reshape(batch * seq, D_MODEL)
    for l in range(depth):
        w = w_in[l]
        c_hg, c_fox = 4 * HG_WIDTH, 3 * FOX_WIDTH
        o_ff = c_hg + c_fox
        o_mq = o_ff + FOX_HEADS
        o_gate = o_mq + MEM_WIDTH
        whg = w[:, :c_hg].astype(BF16)
        wfox = w[:, c_hg:o_ff].astype(BF16)
        wfft = jnp.pad(w[:, o_ff:o_mq].T, ((0, FOX_ROWS - FOX_HEADS), (0, 0))).astype(BF16)
        wmq = w[:, o_mq:o_gate].astype(BF16)
        wgate = w[:, o_gate:].astype(BF16)
        row = lambda v: v.reshape(1, -1).astype(F32)

        zhg, zfox, zmq, zgate, zfft = _in_proj(x2, row(norm_mix_g[l]), whg, wfox, wmq, wgate, wfft,
                                               row(mem_q_norm_g[l]), tm=512)

        ya = _hgrn(zhg, hgrn_lb_logits.astype(F32), row(hgrn_norm_g[l]), batch, seq)

        zfox3 = zfox.reshape(batch, seq, c_fox)
        fb = jnp.pad(fox_f_bias[l].astype(F32), (0, FOX_ROWS - FOX_HEADS)).reshape(FOX_ROWS, 1)
        head_id = jnp.arange(MXU_WIDTH) // FOX_DH
        bd = (head_id[:, None] == head_id[None, :]).astype(BF16) * (1.0 / FOX_DH)
        qa, ka, vt = _fox_prep(zfox3, zfft, fb,
                               jnp.tile(row(fox_q_norm_g[l]), (1, FOX_HEADS)),
                               jnp.tile(row(fox_k_norm_g[l]), (1, FOX_HEADS)), bd, batch, seq)
        qk_bound = (LOG2E * FOX_DH ** 0.5 * jnp.max(jnp.abs(fox_q_norm_g[l])) * jnp.max(jnp.abs(fox_k_norm_g[l])))
        yb = _fox_attn(qk_bound.reshape(1).astype(F32), qa, ka, vt, batch, seq, tq=512)

        mk, mv = _mem_kv(mem.reshape(batch * n_mem, D_MODEL), row(norm_mem_g[l]), mem_kv_w[l].astype(BF16),
                         row(mem_k_norm_g[l]), tm=512)
        wup = ffn_w_up[l]
        out = _merge_ffn(x2.reshape(batch, seq, D_MODEL), ya, yb, zmq.reshape(batch, seq, MEM_WIDTH),
                         mk.reshape(batch, n_mem, MEM_WIDTH), mv.reshape(batch, n_mem, MEM_WIDTH),
                         zgate.reshape(batch, seq, N_BRANCH * D_MODEL),
                         w_br_hgrn[l].astype(BF16), w_br_fox[l].astype(BF16), w_br_mem[l].astype(BF16),
                         w_out[l].astype(BF16), row(norm_ffn_g[l]), wup[:, :D_FF].astype(BF16),
                         wup[:, D_FF:].astype(BF16), ffn_conv_w[l].astype(F32), row(ffn_conv_b[l]),
                         ffn_w_down[l].astype(BF16), tm=512, fc=256)
        x2 = out.reshape(batch * seq, D_MODEL)
    return x2.reshape(batch, seq, D_MODEL)
```
